```python
import jax
import jax.numpy as jnp
from jax import lax
import numpy as np

D_MODEL = 1024
BATCH = 8
SEQ = 2048
DEPTH = 2
DEC_BATCH = 128
DEC_SEQ = 1
PAST_LEN = 2048
PAGE_SIZE = 128

N_EVEN = (DEPTH + 1) // 2
N_ODD = DEPTH // 2

CONV_CH = D_MODEL // 2
CONV_W = 31
SB_HEADS = 8
SB_HEAD_DIM = 64
SB_WIDTH = SB_HEADS * SB_HEAD_DIM
EVEN_IN = 2 * CONV_CH + 3 * SB_WIDTH
EVEN_MIX = CONV_CH + SB_WIDTH
Q_BLOCK = 128
SB_BIAS_INIT = -6.0

RET_HEADS = 8
RET_DK = D_MODEL // RET_HEADS
RET_DV = 2 * RET_DK
RET_QK = RET_HEADS * RET_DK
RET_V = RET_HEADS * RET_DV
ODD_IN = 2 * RET_QK + 2 * RET_V
RET_CHUNK = 128
ROPE_BASE = 10000.0

D_FF = 2816
N_EXPERTS = 8
TOP_K = 2
D_FF_EXPERT = 3584
EPS = 1e-6

kernel_name = 'conv_stickbreak_retention_moe_decoder_step'


def rmsnorm(x, g):
    xf = x.astype(jnp.float32)
    y = xf * lax.rsqrt(jnp.mean(xf * xf, axis=-1, keepdims=True) + EPS)
    return (y * g.astype(jnp.float32)).astype(x.dtype)


def layernorm(x, g, b):
    xf = x.astype(jnp.float32)
    mu = jnp.mean(xf, axis=-1, keepdims=True)
    var = jnp.mean(jnp.square(xf - mu), axis=-1, keepdims=True)
    y = (xf - mu) * lax.rsqrt(var + EPS) * g.astype(jnp.float32) + b.astype(jnp.float32)
    return y.astype(x.dtype)


def swiglu(h, w_gate, w_up, w_down):
    return (jax.nn.silu(h @ w_gate) * (h @ w_up)) @ w_down


def moe_swiglu(h, router_w, w_gate, w_up, w_down):
    logits = (h @ router_w).astype(jnp.float32)
    top_val, top_idx = lax.top_k(logits, TOP_K)
    gates = jax.nn.softmax(top_val, axis=-1)
    combine = jnp.sum(jax.nn.one_hot(top_idx, N_EXPERTS, dtype=jnp.float32) * gates[..., None], axis=-2)
    combine = combine.astype(h.dtype)
    y = jnp.zeros_like(h)
    for e in range(N_EXPERTS):
        y = y + combine[..., e:e + 1] * swiglu(h, w_gate[e], w_up[e], w_down[e])
    return y


def causal_depthwise_conv(u, buf, w, b):
    full = jnp.concatenate([buf.astype(u.dtype), u], axis=1)
    out = lax.conv_general_dilated(full, w[:, None, :].astype(u.dtype), window_strides=(1,), padding='VALID',
                                   dimension_numbers=('NWC', 'WIO', 'NWC'), feature_group_count=u.shape[-1])
    return out + b, full[:, -(CONV_W - 1):]


def stick_breaking_block(q, k, v, q_pos, bias):
    z = jnp.einsum('bqhd,bkhd->bhqk', q, k, preferred_element_type=jnp.float32) * (SB_HEAD_DIM ** -0.5)
    z = z + bias.astype(jnp.float32)[None, :, None, None]
    k_pos = jnp.arange(k.shape[1], dtype=jnp.int32)
    mask = k_pos[None, :] < q_pos[:, None]
    log_keep = jnp.where(mask, jax.nn.log_sigmoid(-z), 0.0)
    suffix = lax.cumsum(log_keep, axis=3, reverse=True)
    log_a = jax.nn.log_sigmoid(z) + (suffix - log_keep)
    a = jnp.where(mask, jnp.exp(log_a), 0.0)
    return jnp.einsum('bhqk,bkhd->bqhd', a.astype(v.dtype), v)


def stick_breaking_prompt(q, k, v, bias):
    b, s, h, d = q.shape
    nb = s // Q_BLOCK
    qb = q.reshape(b, nb, Q_BLOCK, h, d).transpose(1, 0, 2, 3, 4)
    pos = jnp.arange(s, dtype=jnp.int32).reshape(nb, Q_BLOCK)
    out = lax.map(lambda args: stick_breaking_block(args[0], k, v, args[1], bias), (qb, pos))
    return out.transpose(1, 0, 2, 3, 4).reshape(b, s, h, d)


def conv_sb_mixer(h, w_in, conv_w, conv_b, ln_g, ln_b, q_g, k_g, sb_bias, w_out, conv_buf, k_past, v_past):
    b, t, _ = h.shape
    a, gate, q, k, v = jnp.split(h @ w_in, [CONV_CH, 2 * CONV_CH, 2 * CONV_CH + SB_WIDTH,
                                            2 * CONV_CH + 2 * SB_WIDTH], axis=-1)
    u = a * jax.nn.sigmoid(gate)
    c, new_buf = causal_depthwise_conv(u, conv_buf, conv_w, conv_b)
    c = jax.nn.silu(layernorm(c, ln_g, ln_b))
    q = rmsnorm(q.reshape(b, t, SB_HEADS, SB_HEAD_DIM), q_g)
    k = rmsnorm(k.reshape(b, t, SB_HEADS, SB_HEAD_DIM), k_g)
    v = v.reshape(b, t, SB_HEADS, SB_HEAD_DIM)
    if k_past is None:
        o = stick_breaking_prompt(q, k, v, sb_bias)
    else:
        pos = k_past.shape[1] + jnp.arange(t, dtype=jnp.int32)
        k_all = jnp.concatenate([k_past.astype(k.dtype), k], axis=1)
        v_all = jnp.concatenate([v_past.astype(v.dtype), v], axis=1)
        o = stick_breaking_block(q, k_all, v_all, pos, sb_bias)
    mixed = jnp.concatenate([c, o.reshape(b, t, SB_WIDTH)], axis=-1) @ w_out
    return mixed, k, v, new_buf


def rotary(x, pos):
    half = x.shape[-1] // 2
    inv_freq = ROPE_BASE ** (-jnp.linspace(0.0, 1.0, half, dtype=jnp.float32))
    ang = pos.astype(jnp.float32)[:, None] * inv_freq[None, :]
    cos = jnp.cos(ang)[None, :, None, :]
    sin = jnp.sin(ang)[None, :, None, :]
    xf = x.astype(jnp.float32)
    x1, x2 = xf[..., :half], xf[..., half:]
    return jnp.concatenate([x1 * cos - x2 * sin, x1 * sin + x2 * cos], axis=-1).astype(x.dtype)


def retention_chunk(state, qkv):
    q, k, v = qkv
    c = q.shape[1]
    log_g = jnp.log1p(-jnp.exp2(-5.0 - jnp.arange(RET_HEADS, dtype=jnp.float32)))
    idx = jnp.arange(c, dtype=jnp.float32)
    diff = idx[:, None] - idx[None, :]
    decay = jnp.where(diff >= 0, jnp.exp(log_g[:, None, None] * jnp.maximum(diff, 0.0)), 0.0)
    qf, kf, vf = q.astype(jnp.float32), k.astype(jnp.float32), v.astype(jnp.float32)
    scores = jnp.einsum('bnhd,bmhd->bhnm', qf, kf) * decay[None]
    inner = jnp.einsum('bhnm,bmhe->bnhe', scores, vf)
    q_decay = jnp.exp(log_g[None, :] * (idx[:, None] + 1.0))
    cross = jnp.einsum('bnhd,bhde->bnhe', qf, state) * q_decay[None, :, :, None]
    k_decay = jnp.exp(log_g[None, :] * (c - 1.0 - idx[:, None]))
    new_state = (jnp.exp(log_g * c)[None, :, None, None] * state
                 + jnp.einsum('bmhd,mh,bmhe->bhde', kf, k_decay, vf))
    return new_state, inner + cross


def retention_scan(q, k, v, state0):
    b, t = q.shape[:2]
    c = RET_CHUNK if t % RET_CHUNK == 0 else t
    n = t // c

    def chunks(x):
        return x.reshape(b, n, c, *x.shape[2:]).swapaxes(0, 1)

    state, o = lax.scan(retention_chunk, state0.astype(jnp.float32), (chunks(q), chunks(k), chunks(v)))
    return o.swapaxes(0, 1).reshape(b, t, RET_HEADS, RET_DV), state


def retention_mixer(h, w_in, gn_g, gn_b, w_out, pos, state0):
    b, t, _ = h.shape
    q, k, v, g = jnp.split(h @ w_in, [RET_QK, 2 * RET_QK, 2 * RET_QK + RET_V], axis=-1)
    q = rotary(q.reshape(b, t, RET_HEADS, RET_DK), pos)
    k = rotary(k.reshape(b, t, RET_HEADS, RET_DK), pos) * (RET_DK ** -0.5)
    v = v.reshape(b, t, RET_HEADS, RET_DV)
    o, state = retention_scan(q, k, v, state0)
    mu = jnp.mean(o, axis=-1, keepdims=True)
    var = jnp.mean(jnp.square(o - mu), axis=-1, keepdims=True)
    o = ((o - mu) * lax.rsqrt(var + EPS)).reshape(b, t, RET_V) * gn_g.astype(jnp.float32) + gn_b.astype(jnp.float32)
    y = (jax.nn.silu(g) * o.astype(h.dtype)) @ w_out
    return y, state


def setup_inputs(seed: int = 0) -> dict:
    key = jax.random.key(seed)
    ks = iter(jax.random.split(key, 40))

    def nrm(shape, scale):
        return jax.random.normal(next(ks), shape, jnp.float32) * scale

    def gain(shape):
        return 1.0 + nrm(shape, 0.02)

    n_pages = PAST_LEN // PAGE_SIZE
    n_used = DEC_BATCH * n_pages
    n_pool = n_used + max(1, n_used // 4)
    page_table = jax.random.permutation(next(ks), n_pool)[:n_used].reshape(DEC_BATCH, n_pages).astype(jnp.int32)
    return {
        'x_prompt': nrm((BATCH, SEQ, D_MODEL), 1.0),
        'x_sample': nrm((DEC_BATCH, DEC_SEQ, D_MODEL), 1.0),
        'cache_sb_k': nrm((N_EVEN, n_pool, PAGE_SIZE, SB_HEADS, SB_HEAD_DIM), 1.0),
        'cache_sb_v': nrm((N_EVEN, n_pool, PAGE_SIZE, SB_HEADS, SB_HEAD_DIM), 1.0),
        'cache_conv': nrm((N_EVEN, DEC_BATCH, CONV_W - 1, CONV_CH), 0.5),
        'state_ret': nrm((N_ODD, DEC_BATCH, RET_HEADS, RET_DK, RET_DV), 0.5),
        'page_table': page_table,
        'norm_mix_even': gain((N_EVEN, D_MODEL)),
        'w_in_even': nrm((N_EVEN, D_MODEL, EVEN_IN), D_MODEL ** -0.5),
        'conv_w': nrm((N_EVEN, CONV_W, CONV_CH), CONV_W ** -0.5),
        'conv_b': nrm((N_EVEN, CONV_CH), 0.02),
        'conv_ln_g': gain((N_EVEN, CONV_CH)),
        'conv_ln_b': nrm((N_EVEN, CONV_CH), 0.02),
        'sb_q_norm': gain((N_EVEN, SB_HEAD_DIM)),
        'sb_k_norm': gain((N_EVEN, SB_HEAD_DIM)),
        'sb_bias': SB_BIAS_INIT + nrm((N_EVEN, SB_HEADS), 0.5),
        'w_out_even': nrm((N_EVEN, EVEN_MIX, D_MODEL), EVEN_MIX ** -0.5),
        'norm_ffn_even': gain((N_EVEN, D_MODEL)),
        'ffn_w_gate': nrm((N_EVEN, D_MODEL, D_FF), D_MODEL ** -0.5),
        'ffn_w_up': nrm((N_EVEN, D_MODEL, D_FF), D_MODEL ** -0.5),
        'ffn_w_down': nrm((N_EVEN, D_FF, D_MODEL), D_FF ** -0.5),
        'norm_mix_odd': gain((N_ODD, D_MODEL)),
        'w_in_odd': nrm((N_ODD, D_MODEL, ODD_IN), D_MODEL ** -0.5),
        'ret_gn_g': gain((N_ODD, RET_V)),
        'ret_gn_b': nrm((N_ODD, RET_V), 0.02),
        'w_out_odd': nrm((N_ODD, RET_V, D_MODEL), RET_V ** -0.5),
        'norm_ffn_odd': gain((N_ODD, D_MODEL)),
        'router_w': nrm((N_ODD, D_MODEL, N_EXPERTS), D_MODEL ** -0.5),
        'moe_w_gate': nrm((N_ODD, N_EXPERTS, D_MODEL, D_FF_EXPERT), D_MODEL ** -0.5),
        'moe_w_up': nrm((N_ODD, N_EXPERTS, D_MODEL, D_FF_EXPERT), D_MODEL ** -0.5),
        'moe_w_down': nrm((N_ODD, N_EXPERTS, D_FF_EXPERT, D_MODEL), D_FF_EXPERT ** -0.5),
    }


def reference(x_prompt, x_sample, cache_sb_k, cache_sb_v, cache_conv, state_ret, page_table,
              norm_mix_even, w_in_even, conv_w, conv_b, conv_ln_g, conv_ln_b, sb_q_norm, sb_k_norm, sb_bias,
              w_out_even, norm_ffn_even, ffn_w_gate, ffn_w_up, ffn_w_down,
              norm_mix_odd, w_in_odd, ret_gn_g, ret_gn_b, w_out_odd,
              norm_ffn_odd, router_w, moe_w_gate, moe_w_up, moe_w_down):
    n_ctx = page_table.shape[1] * cache_sb_k.shape[2]
    pos_p = jnp.arange(x_prompt.shape[1], dtype=jnp.int32)
    pos_s = n_ctx + jnp.arange(x_sample.shape[1], dtype=jnp.int32)
    yp, ys = x_prompt, x_sample
    kp_l, vp_l, ks_l, vs_l, cp_l, cs_l, rp_l, rs_l = [], [], [], [], [], [], [], []
    for layer in range(DEPTH):
        li = layer // 2
        if layer % 2 == 0:
            even_w = (w_in_even[li], conv_w[li], conv_b[li], conv_ln_g[li], conv_ln_b[li],
                      sb_q_norm[li], sb_k_norm[li], sb_bias[li], w_out_even[li])
            buf0 = jnp.zeros((yp.shape[0], CONV_W - 1, CONV_CH), yp.dtype)
            k_past = cache_sb_k[li][page_table].reshape(ys.shape[0], n_ctx, SB_HEADS, SB_HEAD_DIM)
            v_past = cache_sb_v[li][page_table].reshape(ys.shape[0], n_ctx, SB_HEADS, SB_HEAD_DIM)
            mp, kp, vp, cp = conv_sb_mixer(rmsnorm(yp, norm_mix_even[li]), *even_w, buf0, None, None)
            ms, k_s, v_s, c_s = conv_sb_mixer(rmsnorm(ys, norm_mix_even[li]), *even_w, cache_conv[li], k_past, v_past)
            yp = yp + mp
            ys = ys + ms
            ffn_w = (ffn_w_gate[li], ffn_w_up[li], ffn_w_down[li])
            yp = yp + swiglu(rmsnorm(yp, norm_ffn_even[li]), *ffn_w)
            ys = ys + swiglu(rmsnorm(ys, norm_ffn_even[li]), *ffn_w)
            kp_l.append(kp)
            vp_l.append(vp)
            ks_l.append(k_s)
            vs_l.append(v_s)
            cp_l.append(cp)
            cs_l.append(c_s)
        else:
            ret_w = (w_in_odd[li], ret_gn_g[li], ret_gn_b[li], w_out_odd[li])
            s0 = jnp.zeros((yp.shape[0], RET_HEADS, RET_DK, RET_DV), jnp.float32)
            mp, sp = retention_mixer(rmsnorm(yp, norm_mix_odd[li]), *ret_w, pos_p, s0)
            ms, ss = retention_mixer(rmsnorm(ys, norm_mix_odd[li]), *ret_w, pos_s, state_ret[li])
            yp = yp + mp
            ys = ys + ms
            moe_w = (router_w[li], moe_w_gate[li], moe_w_up[li], moe_w_down[li])
            yp = yp + moe_swiglu(rmsnorm(yp, norm_ffn_odd[li]), *moe_w)
            ys = ys + moe_swiglu(rmsnorm(ys, norm_ffn_odd[li]), *moe_w)
            rp_l.append(sp)
            rs_l.append(ss)
    return (yp, ys, jnp.stack(kp_l), jnp.stack(vp_l), jnp.stack(ks_l), jnp.stack(vs_l),
            jnp.stack(cp_l), jnp.stack(cs_l), jnp.stack(rp_l), jnp.stack(rs_l))
```

```python
import functools

import jax
import jax.numpy as jnp
from jax import lax
from jax.experimental import pallas as pl
from jax.experimental.pallas import tpu as pltpu

F32 = jnp.float32
BF16 = jnp.bfloat16

EPS = 1e-6
RET_HEADS = 8
RET_CHUNK = 128
ROPE_BASE = 10000.0
TOP_K = 2

V7X_VMEM_LIMIT_BYTES = 56 * 1024 * 1024


def _params(*sem):
    return pltpu.CompilerParams(dimension_semantics=sem, vmem_limit_bytes=V7X_VMEM_LIMIT_BYTES)


def _row_tile(m, want):
    t = min(m, want)
    while m % t:
        t //= 2
    return t


def _rms(x, g):
    return x * lax.rsqrt(jnp.mean(x * x, axis=-1, keepdims=True) + EPS) * g


def _split_bf16(x):
    hi = x.astype(BF16)
    lo = (x - hi.astype(F32)).astype(BF16)
    return hi, lo


def _log_sigmoid(z):
    return jnp.minimum(z, 0.0) - jnp.log1p(jnp.exp(-jnp.abs(z)))


def _dot(a, b):
    return jnp.dot(a, b, preferred_element_type=F32)


def _dot_nt(a, b):
    return lax.dot_general(a, b, (((1,), (1,)), ((), ())), preferred_element_type=F32)


def _even_in_kernel(x_ref, g_ref, w_ref, qg_ref, kg_ref, hm_ref,
                    u_ref, k_ref, v_ref, qb_ref, kb_ref, vb_ref, *, c, w, qscale):
    hb = _rms(x_ref[...], g_ref[...]).astype(BF16)

    def proj(lo, width):
        return _dot(hb, w_ref[:, lo:lo + width])

    def head_norm(t, gain):
        hi, lo = _split_bf16(t * t)
        ms = _dot(hi, hm_ref[...]) + _dot(lo, hm_ref[...])
        return t * lax.rsqrt(ms + EPS) * gain

    a = proj(0, c)
    gate = proj(c, c)
    u_ref[...] = a * jax.nn.sigmoid(gate)
    q = head_norm(proj(2 * c, w), qg_ref[...])
    qb_ref[...] = (q * qscale).astype(BF16)
    k = head_norm(proj(2 * c + w, w), kg_ref[...])
    k_ref[...] = k
    kb_ref[...] = k.astype(BF16)
    v = proj(2 * c + 2 * w, w)
    v_ref[...] = v
    vb_ref[...] = v.astype(BF16)


def _even_in(x, g, w_bf, qg_t, kg_t, hm, c, w, dh, tm):
    m, d = x.shape
    n = w_bf.shape[1]
    tm = _row_tile(m, tm)
    row = lambda i: (i, 0)
    fix = lambda i: (0, 0)
    out_f = jax.ShapeDtypeStruct((m, w), F32)
    out_b = jax.ShapeDtypeStruct((m, w), BF16)
    return pl.pallas_call(
        functools.partial(_even_in_kernel, c=c, w=w, qscale=dh ** -0.5),
        grid=(m // tm,),
        in_specs=[pl.BlockSpec((tm, d), row), pl.BlockSpec((1, d), fix), pl.BlockSpec((d, n), fix),
                  pl.BlockSpec((1, w), fix), pl.BlockSpec((1, w), fix), pl.BlockSpec((w, w), fix)],
        out_specs=[pl.BlockSpec((tm, c), row)] + [pl.BlockSpec((tm, w), row)] * 5,
        out_shape=[jax.ShapeDtypeStruct((m, c), F32), out_f, out_f, out_b, out_b, out_b],
        compiler_params=_params("parallel"),
        name="even_in",
    )(x, g, w_bf, qg_t, kg_t, hm)


CONV_HEAD = 32


def _ln_silu(cv, g, b):
    mu = jnp.mean(cv, axis=-1, keepdims=True)
    var = jnp.mean(jnp.square(cv - mu), axis=-1, keepdims=True)
    y = (cv - mu) * lax.rsqrt(var + EPS) * g + b
    return y * jax.nn.sigmoid(y)


def _conv_prompt_kernel(full_ref, w_ref, b_ref, g_ref, beta_ref, o_ref, *, tt, cw):
    t = pl.program_id(1)
    n = tt + CONV_HEAD
    win = full_ref[0, pl.ds(pl.multiple_of(t * tt, tt), n), :]
    acc = jnp.zeros((tt, win.shape[1]), F32)
    for j in range(cw):
        off = j + CONV_HEAD - (cw - 1)
        shifted = win if off == 0 else pltpu.roll(win, n - off, axis=0)
        acc = acc + shifted[:tt] * w_ref[j:j + 1, :]
    o_ref[0] = _ln_silu(acc + b_ref[...], g_ref[...], beta_ref[...])


def _conv_prompt(full, w, b, g, beta, t_len, tt):
    bsz, tf, c = full.shape
    cw = w.shape[0]
    tt = _row_tile(t_len, tt)
    fix = lambda i, t: (0, 0)
    return pl.pallas_call(
        functools.partial(_conv_prompt_kernel, tt=tt, cw=cw),
        grid=(bsz, t_len // tt),
        in_specs=[pl.BlockSpec((1, tf, c), lambda i, t: (i, 0, 0)), pl.BlockSpec((cw, c), fix),
                  pl.BlockSpec((1, c), fix), pl.BlockSpec((1, c), fix), pl.BlockSpec((1, c), fix)],
        out_specs=pl.BlockSpec((1, tt, c), lambda i, t: (i, t, 0)),
        out_shape=jax.ShapeDtypeStruct((bsz, t_len, c), F32),
        compiler_params=_params("parallel", "arbitrary"),
        name="conv_prompt",
    )(full, w, b, g, beta)


def _conv_sample_kernel(buf_ref, u_ref, w_ref, b_ref, g_ref, beta_ref, o_ref, *, cw):
    past = jnp.sum(buf_ref[...] * w_ref[0:cw - 1, :][None], axis=1)
    cv = past + u_ref[...] * w_ref[cw - 1:cw, :] + b_ref[...]
    o_ref[...] = _ln_silu(cv, g_ref[...], beta_ref[...])


def _conv_sample(buf, u, w, b, g, beta):
    s, nb, c = buf.shape
    cw = w.shape[0]
    ts = _row_tile(s, 8)
    fix = lambda i: (0, 0)
    return pl.pallas_call(
        functools.partial(_conv_sample_kernel, cw=cw),
        grid=(s // ts,),
        in_specs=[pl.BlockSpec((ts, nb, c), lambda i: (i, 0, 0)), pl.BlockSpec((ts, c), lambda i: (i, 0)),
                  pl.BlockSpec((cw, c), fix), pl.BlockSpec((1, c), fix), pl.BlockSpec((1, c), fix),
                  pl.BlockSpec((1, c), fix)],
        out_specs=pl.BlockSpec((ts, c), lambda i: (i, 0)),
        out_shape=jax.ShapeDtypeStruct((s, c), F32),
        compiler_params=_params("parallel"),
        name="conv_sample",
    )(buf, u, w, b, g, beta)


def _strict_lower_ones(n):
    r = lax.broadcasted_iota(jnp.int32, (n, n), 0)
    c = lax.broadcasted_iota(jnp.int32, (n, n), 1)
    return jnp.where(r > c, 1.0, 0.0).astype(BF16)


def _suffix_after(lk, ones_after):
    hi, lo = _split_bf16(lk)
    return _dot(hi, ones_after) + _dot(lo, ones_after)


def _sb_prompt_kernel(bias_ref, q_ref, k_ref, v_ref, o_ref, *, tq, heads, dh):
    i = pl.program_id(1)
    row = lax.broadcasted_iota(jnp.int32, (tq, tq), 0)
    col = lax.broadcasted_iota(jnp.int32, (tq, tq), 1)
    ones_after = _strict_lower_ones(tq)
    for h in range(heads):
        lanes = slice(h * dh, (h + 1) * dh)
        qh = q_ref[0, :, lanes]
        bias = bias_ref[h]

        def body(jj, carry, qh=qh, bias=bias, lanes=lanes):
            acc, run = carry
            j = i - jj
            rows = pl.ds(pl.multiple_of(j * tq, tq), tq)
            z = _dot_nt(qh, k_ref[0, rows, lanes]) + bias
            ls = _log_sigmoid(z)
            mask = (row + (i - j) * tq) > col
            lk = jnp.where(mask, ls - z, 0.0)
            a = jnp.where(mask, jnp.exp(ls + _suffix_after(lk, ones_after) + run), 0.0)
            acc = acc + _dot(a.astype(BF16), v_ref[0, rows, lanes])
            return acc, run + jnp.sum(lk, axis=1, keepdims=True)

        acc, _ = lax.fori_loop(0, i + 1, body, (jnp.zeros((tq, dh), F32), jnp.zeros((tq, 1), F32)))
        o_ref[0, :, lanes] = acc


def _sb_prompt(bias, qb, kb, vb, heads, dh, tq):
    bsz, t, w = qb.shape
    tq = _row_tile(t, tq)
    seq = lambda b, i: (b, 0, 0)
    return pl.pallas_call(
        functools.partial(_sb_prompt_kernel, tq=tq, heads=heads, dh=dh),
        grid=(bsz, t // tq),
        in_specs=[pl.BlockSpec(memory_space=pltpu.SMEM),
                  pl.BlockSpec((1, tq, w), lambda b, i: (b, i, 0)),
                  pl.BlockSpec((1, t, w), seq), pl.BlockSpec((1, t, w), seq)],
        out_specs=pl.BlockSpec((1, tq, w), lambda b, i: (b, i, 0)),
        out_shape=jax.ShapeDtypeStruct((bsz, t, w), F32),
        compiler_params=_params("parallel", "arbitrary"),
        name="sb_prompt",
    )(bias, qb, kb, vb)


def _sb_decode_kernel(pt_ref, q_ref, bias_ref, *refs, n_pages, heads, dh):
    del pt_ref
    k_refs = refs[:n_pages]
    v_refs = refs[n_pages:2 * n_pages]
    o_ref = refs[2 * n_pages]
    w = heads * dh
    page = k_refs[0].shape[1]
    head_of_lane = lax.broadcasted_iota(jnp.int32, (heads, w), 1) // dh
    head_mask = head_of_lane == lax.broadcasted_iota(jnp.int32, (heads, w), 0)
    q_bd = jnp.where(head_mask, jnp.broadcast_to(q_ref[0].astype(F32), (heads, w)), 0.0).astype(BF16)
    ones_after = _strict_lower_ones(page)
    acc = jnp.zeros((heads, w), F32)
    run = jnp.zeros((heads, 1), F32)
    for p in reversed(range(n_pages)):
        z = _dot_nt(q_bd, k_refs[p][0].astype(BF16)) + bias_ref[...]
        ls = _log_sigmoid(z)
        lk = ls - z
        a = jnp.exp(ls + _suffix_after(lk, ones_after) + run)
        acc = acc + _dot(a.astype(BF16), v_refs[p][0].astype(BF16))
        run = run + jnp.sum(lk, axis=1, keepdims=True)
    o_ref[0] = jnp.sum(jnp.where(head_mask, acc, 0.0), axis=0, keepdims=True)


def _sb_decode(page_table, qb, bias_col, pool_k, pool_v, heads, dh):
    s, n_pages = page_table.shape
    _, page, w = pool_k.shape
    q3 = qb.reshape(s, 1, w)

    def page_spec(p):
        return pl.BlockSpec((1, page, w), lambda i, pt, p=p: (pt[i, p], 0, 0))

    grid_spec = pltpu.PrefetchScalarGridSpec(
        num_scalar_prefetch=1,
        grid=(s,),
        in_specs=[pl.BlockSpec((1, 1, w), lambda i, pt: (i, 0, 0)),
                  pl.BlockSpec((heads, 1), lambda i, pt: (0, 0))]
        + [page_spec(p) for p in range(n_pages)] * 2,
        out_specs=pl.BlockSpec((1, 1, w), lambda i, pt: (i, 0, 0)),
    )
    out = pl.pallas_call(
        functools.partial(_sb_decode_kernel, n_pages=n_pages, heads=heads, dh=dh),
        grid_spec=grid_spec,
        out_shape=jax.ShapeDtypeStruct((s, 1, w), F32),
        compiler_params=_params("parallel"),
        name="sb_decode",
    )(page_table, q3, bias_col, *([pool_k] * n_pages), *([pool_v] * n_pages))
    return out.reshape(s, w)


def _out_proj_kernel(*refs, n_in):
    res_ref = refs[0]
    o_ref = refs[1 + 2 * n_in]
    acc = res_ref[...]
    for k in range(n_in):
        acc = acc + _dot(refs[1 + k][...].astype(BF16), refs[1 + n_in + k][...])
    o_ref[...] = acc


def _out_proj(res, xs, ws, tm):
    m, d = res.shape
    tm = _row_tile(m, tm)
    row = lambda i: (i, 0)
    fix = lambda i: (0, 0)
    return pl.pallas_call(
        functools.partial(_out_proj_kernel, n_in=len(xs)),
        grid=(m // tm,),
        in_specs=[pl.BlockSpec((tm, d), row)] + [pl.BlockSpec((tm, x.shape[1]), row) for x in xs]
        + [pl.BlockSpec(w.shape, fix) for w in ws],
        out_specs=pl.BlockSpec((tm, d), row),
        out_shape=jax.ShapeDtypeStruct((m, d), F32),
        compiler_params=_params("parallel"),
        name="out_proj",
    )(res, *xs, *ws)


def _ffn_kernel(x_ref, g_ref, comb_ref, wg_ref, wu_ref, wd_ref, o_ref, h_ref, acc_ref, *, routed):
    e = pl.program_id(1)
    j = pl.program_id(2)

    @pl.when((e == 0) & (j == 0))
    def _():
        h_ref[...] = _rms(x_ref[...], g_ref[...]).astype(BF16)
        acc_ref[...] = jnp.zeros_like(acc_ref)

    hb = h_ref[...]
    gate = _dot(hb, wg_ref[0])
    up = _dot(hb, wu_ref[0])
    act = gate * jax.nn.sigmoid(gate) * up
    y = _dot(act.astype(BF16), wd_ref[0])
    if routed:
        lane = lax.broadcasted_iota(jnp.int32, comb_ref.shape, 1)
        y = y * jnp.sum(jnp.where(lane == e, comb_ref[...], 0.0), axis=1, keepdims=True)
    acc_ref[...] += y

    @pl.when((e == pl.num_programs(1) - 1) & (j == pl.num_programs(2) - 1))
    def _():
        o_ref[...] = x_ref[...] + acc_ref[...]


def _ffn(x, g, comb, wg, wu, wd, tm, tf, routed):
    m, d = x.shape
    n_e, _, f = wg.shape
    tm = _row_tile(m, tm)
    tf = _row_tile(f, tf)
    row = lambda i, e, j: (i, 0)
    return pl.pallas_call(
        functools.partial(_ffn_kernel, routed=routed),
        grid=(m // tm, n_e, f // tf),
        in_specs=[pl.BlockSpec((tm, d), row), pl.BlockSpec((1, d), lambda i, e, j: (0, 0)),
                  pl.BlockSpec((tm, comb.shape[1]), row),
                  pl.BlockSpec((1, d, tf), lambda i, e, j: (e, 0, j)),
                  pl.BlockSpec((1, d, tf), lambda i, e, j: (e, 0, j)),
                  pl.BlockSpec((1, tf, d), lambda i, e, j: (e, j, 0))],
        out_specs=pl.BlockSpec((tm, d), row),
        out_shape=jax.ShapeDtypeStruct((m, d), F32),
        scratch_shapes=[pltpu.VMEM((tm, d), BF16), pltpu.VMEM((tm, d), F32)],
        compiler_params=_params("parallel", "arbitrary", "arbitrary"),
        name="moe_ffn" if routed else "ffn",
    )(x, g, comb, wg, wu, wd)


def _router_kernel(x_ref, g_ref, whi_ref, wlo_ref, o_ref, *, n_exp):
    h = _rms(x_ref[...], g_ref[...])
    hi, lo = _split_bf16(h)
    logits = _dot(hi, whi_ref[...]) + (_dot(lo, whi_ref[...]) + _dot(hi, wlo_ref[...]))
    lane = lax.broadcasted_iota(jnp.int32, logits.shape, 1).astype(F32)
    neg = jnp.float32(-jnp.inf)
    big = jnp.float32(logits.shape[1])
    l1 = jnp.where(lane < n_exp, logits, neg)
    m1 = jnp.max(l1, axis=1, keepdims=True)
    i1 = jnp.min(jnp.where(l1 == m1, lane, big), axis=1, keepdims=True)
    l2 = jnp.where(lane == i1, neg, l1)
    m2 = jnp.max(l2, axis=1, keepdims=True)
    i2 = jnp.min(jnp.where(l2 == m2, lane, big), axis=1, keepdims=True)
    e2 = jnp.exp(m2 - m1)
    den = 1.0 + e2
    o_ref[...] = jnp.where(lane == i1, 1.0 / den, 0.0) + jnp.where(lane == i2, e2 / den, 0.0)


def _router(x, g, w_hi, w_lo, n_exp, tm):
    m, d = x.shape
    lanes = w_hi.shape[1]
    tm = _row_tile(m, tm)
    row = lambda i: (i, 0)
    fix = lambda i: (0, 0)
    return pl.pallas_call(
        functools.partial(_router_kernel, n_exp=n_exp),
        grid=(m // tm,),
        in_specs=[pl.BlockSpec((tm, d), row), pl.BlockSpec((1, d), fix),
                  pl.BlockSpec((d, lanes), fix), pl.BlockSpec((d, lanes), fix)],
        out_specs=pl.BlockSpec((tm, lanes), row),
        out_shape=jax.ShapeDtypeStruct((m, lanes), F32),
        compiler_params=_params("parallel"),
        name="router",
    )(x, g, w_hi, w_lo)


def _norm_proj_kernel(x_ref, g_ref, w_ref, o_ref, h_ref):
    @pl.when(pl.program_id(1) == 0)
    def _():
        h_ref[...] = _rms(x_ref[...], g_ref[...]).astype(BF16)

    o_ref[...] = _dot(h_ref[...], w_ref[...])


def _norm_proj(x, g, w_bf, tm, tn):
    m, d = x.shape
    n = w_bf.shape[1]
    tm = _row_tile(m, tm)
    tn = _row_tile(n, tn)
    return pl.pallas_call(
        _norm_proj_kernel,
        grid=(m // tm, n // tn),
        in_specs=[pl.BlockSpec((tm, d), lambda i, j: (i, 0)), pl.BlockSpec((1, d), lambda i, j: (0, 0)),
                  pl.BlockSpec((d, tn), lambda i, j: (0, j))],
        out_specs=pl.BlockSpec((tm, tn), lambda i, j: (i, j)),
        out_shape=jax.ShapeDtypeStruct((m, n), F32),
        scratch_shapes=[pltpu.VMEM((tm, d), BF16)],
        compiler_params=_params("parallel", "arbitrary"),
        name="norm_proj",
    )(x, g, w_bf)


def _rotate(x, cos, sin_signed):
    return x * cos + pltpu.roll(x, x.shape[1] // 2, axis=1) * sin_signed


def _group_norm_gate(o, gate, gn_g, gn_b):
    mu = jnp.mean(o, axis=-1, keepdims=True)
    var = jnp.mean(jnp.square(o - mu), axis=-1, keepdims=True)
    on = (o - mu) * lax.rsqrt(var + EPS) * gn_g + gn_b
    return gate * jax.nn.sigmoid(gate) * on


def _ret_prompt_kernel(gc_ref, q_ref, k_ref, v_ref, gate_ref, cos_ref, sin_ref, dec_ref, qdec_ref, kdec_ref,
                       gng_ref, gnb_ref, y_ref, s_ref, *, heads, dk, dv):
    @pl.when(pl.program_id(1) == 0)
    def _():
        s_ref[...] = jnp.zeros_like(s_ref)

    cos = cos_ref[...]
    sin = sin_ref[...]
    for h in range(heads):
        kl = slice(h * dk, (h + 1) * dk)
        vl = slice(h * dv, (h + 1) * dv)
        qr = _rotate(q_ref[:, kl], cos, sin)
        kr = _rotate(k_ref[:, kl], cos, sin) * (dk ** -0.5)
        qb = qr.astype(BF16)
        vb = v_ref[:, vl].astype(BF16)
        state = s_ref[0, h]
        scores = _dot_nt(qb, kr.astype(BF16)) * dec_ref[h]
        o = _dot(scores.astype(BF16), vb) + _dot(qb, state.astype(BF16)) * qdec_ref[h]
        kd_t = (kr * kdec_ref[h]).T.astype(BF16)
        s_ref[0, h] = gc_ref[h] * state + _dot(kd_t, vb)
        y_ref[:, vl] = _group_norm_gate(o, gate_ref[:, vl], gng_ref[:, vl], gnb_ref[:, vl])


def _ret_prompt(proj, bsz, t, gc, cos, sin, dec, qdec, kdec, gn_g, gn_b, heads, dk, dv):
    m = proj.shape[0]
    c = dec.shape[1]
    nc = t // c
    qk, vw = heads * dk, heads * dv
    assert vw == 2 * qk
    tok = lambda col: (lambda b, ci: (b * nc + ci, col))
    fix2 = lambda b, ci: (0, 0)
    fix3 = lambda b, ci: (0, 0, 0)
    return pl.pallas_call(
        functools.partial(_ret_prompt_kernel, heads=heads, dk=dk, dv=dv),
        grid=(bsz, nc),
        in_specs=[pl.BlockSpec(memory_space=pltpu.SMEM),
                  pl.BlockSpec((c, qk), tok(0)), pl.BlockSpec((c, qk), tok(1)),
                  pl.BlockSpec((c, vw), tok(1)), pl.BlockSpec((c, vw), tok(2)),
                  pl.BlockSpec((c, dk), lambda b, ci: (ci, 0)), pl.BlockSpec((c, dk), lambda b, ci: (ci, 0)),
                  pl.BlockSpec(dec.shape, fix3), pl.BlockSpec(qdec.shape, fix3), pl.BlockSpec(kdec.shape, fix3),
                  pl.BlockSpec((1, vw), fix2), pl.BlockSpec((1, vw), fix2)],
        out_specs=[pl.BlockSpec((c, vw), lambda b, ci: (b * nc + ci, 0)),
                   pl.BlockSpec((1, heads, dk, dv), lambda b, ci: (b, 0, 0, 0))],
        out_shape=[jax.ShapeDtypeStruct((m, vw), F32), jax.ShapeDtypeStruct((bsz, heads, dk, dv), F32)],
        compiler_params=_params("parallel", "arbitrary"),
        name="ret_prompt",
    )(gc, proj, proj, proj, proj, cos, sin, dec, qdec, kdec, gn_g, gn_b)


def _ret_sample_kernel(gamma_ref, p_ref, s_ref, cos_ref, sin_ref, gng_ref, gnb_ref, y_ref, so_ref,
                       *, heads, dk, dv):
    qk, vw = heads * dk, heads * dv
    cos = cos_ref[...]
    sin = sin_ref[...]

    def column(x):
        col = jnp.broadcast_to(x, (dk, dk)).T
        return jnp.concatenate([col] * (dv // dk), axis=1)

    for h in range(heads):
        q = _rotate(p_ref[0, :, h * dk:(h + 1) * dk], cos, sin)
        k = _rotate(p_ref[0, :, qk + h * dk:qk + (h + 1) * dk], cos, sin) * (dk ** -0.5)
        vl = slice(h * dv, (h + 1) * dv)
        v = p_ref[0, :, 2 * qk + h * dv:2 * qk + (h + 1) * dv]
        gate = p_ref[0, :, 2 * qk + vw + h * dv:2 * qk + vw + (h + 1) * dv]
        state = s_ref[0, h]
        o = jnp.sum(q * k, axis=-1, keepdims=True) * v \
            + jnp.sum(column(q) * state, axis=0, keepdims=True) * gamma_ref[h]
        so_ref[0, h] = gamma_ref[h] * state + column(k) * v
        y_ref[0, :, vl] = _group_norm_gate(o, gate, gng_ref[:, vl], gnb_ref[:, vl])


def _ret_sample(proj, state, gamma, cos, sin, gn_g, gn_b, heads, dk, dv):
    s, n = proj.shape
    vw = heads * dv
    p3 = proj.reshape(s, 1, n)
    fix2 = lambda i: (0, 0)
    y, new_state = pl.pallas_call(
        functools.partial(_ret_sample_kernel, heads=heads, dk=dk, dv=dv),
        grid=(s,),
        in_specs=[pl.BlockSpec(memory_space=pltpu.SMEM),
                  pl.BlockSpec((1, 1, n), lambda i: (i, 0, 0)),
                  pl.BlockSpec((1, heads, dk, dv), lambda i: (i, 0, 0, 0)),
                  pl.BlockSpec((1, dk), fix2), pl.BlockSpec((1, dk), fix2),
                  pl.BlockSpec((1, vw), fix2), pl.BlockSpec((1, vw), fix2)],
        out_specs=[pl.BlockSpec((1, 1, vw), lambda i: (i, 0, 0)),
                   pl.BlockSpec((1, heads, dk, dv), lambda i: (i, 0, 0, 0))],
        out_shape=[jax.ShapeDtypeStruct((s, 1, vw), F32), jax.ShapeDtypeStruct(state.shape, F32)],
        compiler_params=_params("parallel"),
        name="ret_sample",
    )(gamma, p3, state, cos, sin, gn_g, gn_b)
    return y.reshape(s, vw), new_state


def _rope_tables(pos, dk):
    half = dk // 2
    inv_freq = ROPE_BASE ** (-jnp.linspace(0.0, 1.0, half, dtype=F32))
    ang = pos.astype(F32)[:, None] * inv_freq[None, :]
    cos, sin = jnp.cos(ang), jnp.sin(ang)
    return jnp.concatenate([cos, cos], axis=1), jnp.concatenate([-sin, sin], axis=1)


def _decay_tables(c, dk, dv):
    log_g = jnp.log1p(-jnp.exp2(-5.0 - jnp.arange(RET_HEADS, dtype=F32)))
    idx = jnp.arange(c, dtype=F32)
    diff = idx[:, None] - idx[None, :]
    dec = jnp.where(diff >= 0, jnp.exp(log_g[:, None, None] * jnp.maximum(diff, 0.0)), 0.0)
    qdec = jnp.exp(log_g[:, None] * (idx[None, :] + 1.0))
    kdec = jnp.exp(log_g[:, None] * (c - 1.0 - idx[None, :]))
    qdec = jnp.broadcast_to(qdec[:, :, None], (RET_HEADS, c, dv))
    kdec = jnp.broadcast_to(kdec[:, :, None], (RET_HEADS, c, dk))
    return dec, qdec, kdec, jnp.exp(log_g * c), jnp.exp(log_g)


def kernel(x_prompt, x_sample, cache_sb_k, cache_sb_v, cache_conv, state_ret, page_table, norm_mix_even, w_in_even, conv_w, conv_b, conv_ln_g, conv_ln_b, sb_q_norm, sb_k_norm, sb_bias, w_out_even, norm_ffn_even, ffn_w_gate, ffn_w_up, ffn_w_down, norm_mix_odd, w_in_odd, ret_gn_g, ret_gn_b, w_out_odd, norm_ffn_odd, router_w, moe_w_gate, moe_w_up, moe_w_down):
    bsz, t, d = x_prompt.shape
    s = x_sample.shape[0]
    assert x_sample.shape[1] == 1
    n_layers = w_in_even.shape[0] + w_in_odd.shape[0]
    heads, dh = sb_bias.shape[1], sb_q_norm.shape[1]
    w = heads * dh
    c = conv_w.shape[2]
    cw = conv_w.shape[1]
    page = cache_sb_k.shape[2]
    n_ctx = page_table.shape[1] * page
    dk = d // RET_HEADS
    dv = 2 * dk
    n_exp = router_w.shape[2]

    yp = x_prompt.reshape(bsz * t, d)
    ys = x_sample.reshape(s, d)
    row = lambda a: a.reshape(1, -1)

    head_mean = jnp.kron(jnp.eye(heads, dtype=F32), jnp.full((dh, dh), 1.0 / dh, F32)).astype(BF16)
    cos_p, sin_p = _rope_tables(jnp.arange(t, dtype=jnp.int32), dk)
    cos_s, sin_s = _rope_tables(jnp.full((1,), n_ctx, jnp.int32), dk)
    chunk = RET_CHUNK if t % RET_CHUNK == 0 else t
    dec, qdec, kdec, gamma_chunk, gamma = _decay_tables(chunk, dk, dv)

    kp_l, vp_l, ks_l, vs_l, cp_l, cs_l, rp_l, rs_l = [], [], [], [], [], [], [], []
    for layer in range(n_layers):
        li = layer // 2
        if layer % 2 == 0:
            w_in = w_in_even[li].astype(BF16)
            qg = row(jnp.tile(sb_q_norm[li], heads))
            kg = row(jnp.tile(sb_k_norm[li], heads))
            w_out = w_out_even[li].astype(BF16)
            conv_args = (conv_w[li], row(conv_b[li]), row(conv_ln_g[li]), row(conv_ln_b[li]))
            g_mix = row(norm_mix_even[li])

            up, kp, vp, qpb, kpb, vpb = _even_in(yp, g_mix, w_in, qg, kg, head_mean, c, w, dh, 512)
            us, k_s, v_s, qsb, _, _ = _even_in(ys, g_mix, w_in, qg, kg, head_mean, c, w, dh, 128)

            up3 = up.reshape(bsz, t, c)
            full = jnp.pad(up3, ((0, 0), (CONV_HEAD, 0), (0, 0)))
            cp = _conv_prompt(full, *conv_args, t, 256).reshape(bsz * t, c)
            cs = _conv_sample(cache_conv[li], us, *conv_args)

            op = _sb_prompt(sb_bias[li], qpb.reshape(bsz, t, w), kpb.reshape(bsz, t, w),
                            vpb.reshape(bsz, t, w), heads, dh, 256).reshape(bsz * t, w)
            n_pool = cache_sb_k.shape[1]
            os_ = _sb_decode(page_table, qsb, sb_bias[li].reshape(heads, 1),
                             cache_sb_k[li].reshape(n_pool, page, w), cache_sb_v[li].reshape(n_pool, page, w),
                             heads, dh)

            yp = _out_proj(yp, [cp, op], [w_out[:c], w_out[c:]], 1024)
            ys = _out_proj(ys, [cs, os_], [w_out[:c], w_out[c:]], 128)

            ffn_w = (ffn_w_gate[li].astype(BF16)[None], ffn_w_up[li].astype(BF16)[None],
                     ffn_w_down[li].astype(BF16)[None])
            g_ffn = row(norm_ffn_even[li])
            yp = _ffn(yp, g_ffn, jnp.zeros((bsz * t, 128), F32), *ffn_w, 1024, 256, False)
            ys = _ffn(ys, g_ffn, jnp.zeros((s, 128), F32), *ffn_w, 128, 256, False)

            kp_l.append(kp.reshape(bsz, t, heads, dh))
            vp_l.append(vp.reshape(bsz, t, heads, dh))
            ks_l.append(k_s.reshape(s, 1, heads, dh))
            vs_l.append(v_s.reshape(s, 1, heads, dh))
            cp_l.append(up3[:, t - (cw - 1):])
            cs_l.append(jnp.concatenate([cache_conv[li][:, 1:], us[:, None, :]], axis=1))
        else:
            w_in = w_in_odd[li].astype(BF16)
            w_out = w_out_odd[li].astype(BF16)
            g_mix = row(norm_mix_odd[li])
            gn_g, gn_b = row(ret_gn_g[li]), row(ret_gn_b[li])

            pp = _norm_proj(yp, g_mix, w_in, 1024, 2048)
            ps = _norm_proj(ys, g_mix, w_in, 128, 2048)
            mp, sp = _ret_prompt(pp, bsz, t, gamma_chunk, cos_p, sin_p, dec, qdec, kdec, gn_g, gn_b,
                                 RET_HEADS, dk, dv)
            ms, ss = _ret_sample(ps, state_ret[li], gamma, cos_s, sin_s, gn_g, gn_b, RET_HEADS, dk, dv)
            yp = _out_proj(yp, [mp], [w_out], 1024)
            ys = _out_proj(ys, [ms], [w_out], 128)

            g_ffn = row(norm_ffn_odd[li])
            lanes = 128
            rw = jnp.pad(router_w[li], ((0, 0), (0, lanes - n_exp)))
            rw_hi = rw.astype(BF16)
            rw_lo = (rw - rw_hi.astype(F32)).astype(BF16)
            moe_w = (moe_w_gate[li].astype(BF16), moe_w_up[li].astype(BF16), moe_w_down[li].astype(BF16))
            comb_p = _router(yp, g_ffn, rw_hi, rw_lo, n_exp, 1024)
            comb_s = _router(ys, g_ffn, rw_hi, rw_lo, n_exp, 128)
            yp = _ffn(yp, g_ffn, comb_p, *moe_w, 1024, 512, True)
            ys = _ffn(ys, g_ffn, comb_s, *moe_w, 128, 512, True)
            rp_l.append(sp)
            rs_l.append(ss)

    return (yp.reshape(bsz, t, d), ys.reshape(s, 1, d), jnp.stack(kp_l), jnp.stack(vp_l), jnp.stack(ks_l),
            jnp.stack(vs_l), jnp.stack(cp_l), jnp.stack(cs_l), jnp.stack(rp_l), jnp.stack(rs_l))
```

```python
import functools

import jax
import jax.numpy as jnp
from jax import lax
from jax.experimental import pallas as pl
from jax.experimental.pallas import tpu as pltpu

F32 = jnp.float32
BF16 = jnp.bfloat16

EPS = 1e-6
RET_HEADS = 8
RET_CHUNK = 128
ROPE_BASE = 10000.0
TOP_K = 2

V7X_VMEM_LIMIT_BYTES = 56 * 1024 * 1024


def _params(*sem):
    return pltpu.CompilerParams(dimension_semantics=sem, vmem_limit_bytes=V7X_VMEM_LIMIT_BYTES)


def _row_tile(m, want):
    t = min(m, want)
    while m % t:
        t //= 2
    return t


def _rms(x, g):
    return x * lax.rsqrt(jnp.mean(x * x, axis=-1, keepdims=True) + EPS) * g


def _split_bf16(x):
    hi = x.astype(BF16)
    lo = (x - hi.astype(F32)).astype(BF16)
    return hi, lo


def _log_sigmoid(z):
    return jnp.minimum(z, 0.0) - jnp.log(1.0 + jnp.exp(-jnp.abs(z)))


def _dot(a, b):
    return jnp.dot(a, b, preferred_element_type=F32)


def _dot_nt(a, b):
    return lax.dot_general(a, b, (((1,), (1,)), ((), ())), preferred_element_type=F32)


def _even_in_kernel(x_ref, g_ref, w_ref, qg_ref, kg_ref, hm_ref,
                    u_ref, k_ref, v_ref, qb_ref, kb_ref, vb_ref, *, c, w, qscale):
    hb = _rms(x_ref[...], g_ref[...]).astype(BF16)

    def proj(lo, width):
        return _dot(hb, w_ref[:, lo:lo + width])

    def head_norm(t, gain):
        hi, lo = _split_bf16(t * t)
        ms = _dot(hi, hm_ref[...]) + _dot(lo, hm_ref[...])
        return t * lax.rsqrt(ms + EPS) * gain

    a = proj(0, c)
    gate = proj(c, c)
    u_ref[...] = a * jax.nn.sigmoid(gate)
    q = head_norm(proj(2 * c, w), qg_ref[...])
    qb_ref[...] = (q * qscale).astype(BF16)
    k = head_norm(proj(2 * c + w, w), kg_ref[...])
    k_ref[...] = k
    kb_ref[...] = k.astype(BF16)
    v = proj(2 * c + 2 * w, w)
    v_ref[...] = v
    vb_ref[...] = v.astype(BF16)


def _even_in(x, g, w_bf, qg_t, kg_t, hm, c, w, dh, tm):
    m, d = x.shape
    n = w_bf.shape[1]
    tm = _row_tile(m, tm)
    row = lambda i: (i, 0)
    fix = lambda i: (0, 0)
    out_f = jax.ShapeDtypeStruct((m, w), F32)
    out_b = jax.ShapeDtypeStruct((m, w), BF16)
    return pl.pallas_call(
        functools.partial(_even_in_kernel, c=c, w=w, qscale=dh ** -0.5),
        grid=(m // tm,),
        in_specs=[pl.BlockSpec((tm, d), row), pl.BlockSpec((1, d), fix), pl.BlockSpec((d, n), fix),
                  pl.BlockSpec((1, w), fix), pl.BlockSpec((1, w), fix), pl.BlockSpec((w, w), fix)],
        out_specs=[pl.BlockSpec((tm, c), row)] + [pl.BlockSpec((tm, w), row)] * 5,
        out_shape=[jax.ShapeDtypeStruct((m, c), F32), out_f, out_f, out_b, out_b, out_b],
        compiler_params=_params("parallel"),
        name="even_in",
    )(x, g, w_bf, qg_t, kg_t, hm)


CONV_HEAD = 32


def _ln_silu(cv, g, b):
    mu = jnp.mean(cv, axis=-1, keepdims=True)
    var = jnp.mean(jnp.square(cv - mu), axis=-1, keepdims=True)
    y = (cv - mu) * lax.rsqrt(var + EPS) * g + b
    return y * jax.nn.sigmoid(y)


def _conv_prompt_kernel(full_ref, w_ref, b_ref, g_ref, beta_ref, o_ref, *, tt, cw):
    t = pl.program_id(1)
    n = tt + CONV_HEAD
    win = full_ref[0, pl.ds(pl.multiple_of(t * tt, tt), n), :]
    acc = jnp.zeros((tt, win.shape[1]), F32)
    for j in range(cw):
        off = j + CONV_HEAD - (cw - 1)
        shifted = win if off == 0 else pltpu.roll(win, n - off, axis=0)
        acc = acc + shifted[:tt] * w_ref[j:j + 1, :]
    o_ref[0] = _ln_silu(acc + b_ref[...], g_ref[...], beta_ref[...])


def _conv_prompt(full, w, b, g, beta, t_len, tt):
    bsz, tf, c = full.shape
    cw = w.shape[0]
    tt = _row_tile(t_len, tt)
    fix = lambda i, t: (0, 0)
    return pl.pallas_call(
        functools.partial(_conv_prompt_kernel, tt=tt, cw=cw),
        grid=(bsz, t_len // tt),
        in_specs=[pl.BlockSpec((1, tf, c), lambda i, t: (i, 0, 0)), pl.BlockSpec((cw, c), fix),
                  pl.BlockSpec((1, c), fix), pl.BlockSpec((1, c), fix), pl.BlockSpec((1, c), fix)],
        out_specs=pl.BlockSpec((1, tt, c), lambda i, t: (i, t, 0)),
        out_shape=jax.ShapeDtypeStruct((bsz, t_len, c), F32),
        compiler_params=_params("parallel", "arbitrary"),
        name="conv_prompt",
    )(full, w, b, g, beta)


def _conv_sample_kernel(buf_ref, u_ref, w_ref, b_ref, g_ref, beta_ref, o_ref, *, cw):
    past = jnp.sum(buf_ref[...] * w_ref[0:cw - 1, :][None], axis=1)
    cv = past + u_ref[...] * w_ref[cw - 1:cw, :] + b_ref[...]
    o_ref[...] = _ln_silu(cv, g_ref[...], beta_ref[...])


def _conv_sample(buf, u, w, b, g, beta):
    s, nb, c = buf.shape
    cw = w.shape[0]
    ts = _row_tile(s, 8)
    fix = lambda i: (0, 0)
    return pl.pallas_call(
        functools.partial(_conv_sample_kernel, cw=cw),
        grid=(s // ts,),
        in_specs=[pl.BlockSpec((ts, nb, c), lambda i: (i, 0, 0)), pl.BlockSpec((ts, c), lambda i: (i, 0)),
                  pl.BlockSpec((cw, c), fix), pl.BlockSpec((1, c), fix), pl.BlockSpec((1, c), fix),
                  pl.BlockSpec((1, c), fix)],
        out_specs=pl.BlockSpec((ts, c), lambda i: (i, 0)),
        out_shape=jax.ShapeDtypeStruct((s, c), F32),
        compiler_params=_params("parallel"),
        name="conv_sample",
    )(buf, u, w, b, g, beta)


def _strict_lower_ones(n):
    r = lax.broadcasted_iota(jnp.int32, (n, n), 0)
    c = lax.broadcasted_iota(jnp.int32, (n, n), 1)
    return jnp.where(r > c, 1.0, 0.0).astype(BF16)


def _suffix_after(lk, ones_after):
    hi, lo = _split_bf16(lk)
    return _dot(hi, ones_after) + _dot(lo, ones_after)


def _sb_block(qh, k_blk, v_blk, bias, run, ones_after, mask):
    z = _dot_nt(qh, k_blk) + bias
    ls = _log_sigmoid(z)
    lk = ls - z
    if mask is not None:
        lk = jnp.where(mask, lk, 0.0)
    a = jnp.exp(ls + _suffix_after(lk, ones_after) + run)
    if mask is not None:
        a = jnp.where(mask, a, 0.0)
    return _dot(a.astype(BF16), v_blk), run + jnp.sum(lk, axis=1, keepdims=True)


def _sb_prompt_kernel(bias_ref, q_ref, k_ref, v_ref, o_ref, *, tq, heads, dh, group):
    i = pl.program_id(1)
    row = lax.broadcasted_iota(jnp.int32, (tq, tq), 0)
    col = lax.broadcasted_iota(jnp.int32, (tq, tq), 1)
    causal = row > col
    ones_after = _strict_lower_ones(tq)
    for h0 in range(0, heads, group):
        lanes = [slice(h * dh, (h + 1) * dh) for h in range(h0, h0 + group)]
        qs = [q_ref[0, :, l] for l in lanes]
        biases = [bias_ref[h] for h in range(h0, h0 + group)]

        def tile(j, carry, mask, lanes=lanes, qs=qs, biases=biases):
            rows = pl.ds(pl.multiple_of(j * tq, tq), tq)
            out = []
            for g in range(group):
                acc, run = carry[g]
                pv, run = _sb_block(qs[g], k_ref[0, rows, lanes[g]], v_ref[0, rows, lanes[g]], biases[g],
                                    run, ones_after, mask)
                out.append((acc + pv, run))
            return tuple(out)

        init = tuple((jnp.zeros((tq, dh), F32), jnp.zeros((tq, 1), F32)) for _ in range(group))
        carry = tile(i, init, causal)
        carry = lax.fori_loop(1, i + 1, lambda jj, c, tile=tile: tile(i - jj, c, None), carry)
        for g in range(group):
            o_ref[0, :, lanes[g]] = carry[g][0]


def _sb_prompt(bias, qb, kb, vb, heads, dh, tq):
    bsz, t, w = qb.shape
    tq = _row_tile(t, tq)
    seq = lambda b, i: (b, 0, 0)
    return pl.pallas_call(
        functools.partial(_sb_prompt_kernel, tq=tq, heads=heads, dh=dh, group=2),
        grid=(bsz, t // tq),
        in_specs=[pl.BlockSpec(memory_space=pltpu.SMEM),
                  pl.BlockSpec((1, tq, w), lambda b, i: (b, i, 0)),
                  pl.BlockSpec((1, t, w), seq), pl.BlockSpec((1, t, w), seq)],
        out_specs=pl.BlockSpec((1, tq, w), lambda b, i: (b, i, 0)),
        out_shape=jax.ShapeDtypeStruct((bsz, t, w), F32),
        compiler_params=_params("parallel", "arbitrary"),
        name="sb_prompt",
    )(bias, qb, kb, vb)


def _sb_decode_kernel(pt_ref, q_ref, bias_ref, *refs, n_pages, heads, dh):
    del pt_ref
    k_refs = refs[:n_pages]
    v_refs = refs[n_pages:2 * n_pages]
    o_ref = refs[2 * n_pages]
    w = heads * dh
    page = k_refs[0].shape[2]
    rows = n_pages * heads
    head_of_lane = lax.broadcasted_iota(jnp.int32, (heads, w), 1) // dh
    head_mask = head_of_lane == lax.broadcasted_iota(jnp.int32, (heads, w), 0)
    q_bd = jnp.where(head_mask, jnp.broadcast_to(q_ref[0].astype(F32), (heads, w)), 0.0).astype(BF16)
    z = jnp.concatenate([_dot(q_bd, k_refs[p][0].astype(BF16)) for p in range(n_pages)], axis=0)
    z = z + bias_ref[...]
    ls = _log_sigmoid(z)
    lk = ls - z
    hi, lo = _split_bf16(lk)
    ones_after = _strict_lower_ones(page)
    within = _dot(hi, ones_after) + _dot(lo, ones_after)
    all_ones = jnp.ones((page, page), BF16)
    t_hi, t_lo = _split_bf16(_dot(hi, all_ones) + _dot(lo, all_ones))
    r = lax.broadcasted_iota(jnp.int32, (rows, rows), 0)
    c = lax.broadcasted_iota(jnp.int32, (rows, rows), 1)
    later_page = jnp.where((r % heads == c % heads) & (c // heads > r // heads), 1.0, 0.0).astype(BF16)
    run = _dot(later_page, t_hi) + _dot(later_page, t_lo)
    a = jnp.exp(ls + within + run).astype(BF16)
    acc = jnp.zeros((heads, w), F32)
    for p in range(n_pages):
        acc = acc + _dot_nt(a[p * heads:(p + 1) * heads], v_refs[p][0].astype(BF16))
    o_ref[0] = jnp.sum(jnp.where(head_mask, acc, 0.0), axis=0, keepdims=True)


def _sb_decode(page_table, qb, bias_rows, pool_kt, pool_vt, heads, dh):
    s, n_pages = page_table.shape
    _, w, page = pool_kt.shape
    q3 = qb.reshape(s, 1, w)

    def page_spec(p):
        return pl.BlockSpec((1, w, page), lambda i, pt, p=p: (pt[i, p], 0, 0))

    grid_spec = pltpu.PrefetchScalarGridSpec(
        num_scalar_prefetch=1,
        grid=(s,),
        in_specs=[pl.BlockSpec((1, 1, w), lambda i, pt: (i, 0, 0)),
                  pl.BlockSpec((n_pages * heads, 1), lambda i, pt: (0, 0))]
        + [page_spec(p) for p in range(n_pages)] * 2,
        out_specs=pl.BlockSpec((1, 1, w), lambda i, pt: (i, 0, 0)),
    )
    out = pl.pallas_call(
        functools.partial(_sb_decode_kernel, n_pages=n_pages, heads=heads, dh=dh),
        grid_spec=grid_spec,
        out_shape=jax.ShapeDtypeStruct((s, 1, w), F32),
        compiler_params=_params("parallel"),
        name="sb_decode",
    )(page_table, q3, bias_rows, *([pool_kt] * n_pages), *([pool_vt] * n_pages))
    return out.reshape(s, w)


def _out_proj_kernel(*refs, n_in):
    res_ref = refs[0]
    o_ref = refs[1 + 2 * n_in]
    acc = res_ref[...]
    for k in range(n_in):
        acc = acc + _dot(refs[1 + k][...].astype(BF16), refs[1 + n_in + k][...])
    o_ref[...] = acc


def _out_proj(res, xs, ws, tm):
    m, d = res.shape
    tm = _row_tile(m, tm)
    row = lambda i: (i, 0)
    fix = lambda i: (0, 0)
    return pl.pallas_call(
        functools.partial(_out_proj_kernel, n_in=len(xs)),
        grid=(m // tm,),
        in_specs=[pl.BlockSpec((tm, d), row)] + [pl.BlockSpec((tm, x.shape[1]), row) for x in xs]
        + [pl.BlockSpec(w.shape, fix) for w in ws],
        out_specs=pl.BlockSpec((tm, d), row),
        out_shape=jax.ShapeDtypeStruct((m, d), F32),
        compiler_params=_params("parallel"),
        name="out_proj",
    )(res, *xs, *ws)


def _swiglu_step(hb, wg, wu, wd):
    gate = _dot(hb, wg)
    act = gate * jax.nn.sigmoid(gate) * _dot(hb, wu)
    return _dot(act.astype(BF16), wd)


def _ffn_kernel(x_ref, g_ref, wg_ref, wu_ref, wd_ref, o_ref, h_ref, acc_ref):
    j = pl.program_id(1)

    @pl.when(j == 0)
    def _():
        h_ref[...] = _rms(x_ref[...], g_ref[...]).astype(BF16)
        acc_ref[...] = jnp.zeros_like(acc_ref)

    acc_ref[...] += _swiglu_step(h_ref[...], wg_ref[...], wu_ref[...], wd_ref[...])

    @pl.when(j == pl.num_programs(1) - 1)
    def _():
        o_ref[...] = x_ref[...] + acc_ref[...]


def _ffn(x, g, wg, wu, wd, tm, tf):
    m, d = x.shape
    f = wg.shape[1]
    tm = _row_tile(m, tm)
    tf = _row_tile(f, tf)
    row = lambda i, j: (i, 0)
    return pl.pallas_call(
        _ffn_kernel,
        grid=(m // tm, f // tf),
        in_specs=[pl.BlockSpec((tm, d), row), pl.BlockSpec((1, d), lambda i, j: (0, 0)),
                  pl.BlockSpec((d, tf), lambda i, j: (0, j)), pl.BlockSpec((d, tf), lambda i, j: (0, j)),
                  pl.BlockSpec((tf, d), lambda i, j: (j, 0))],
        out_specs=pl.BlockSpec((tm, d), row),
        out_shape=jax.ShapeDtypeStruct((m, d), F32),
        scratch_shapes=[pltpu.VMEM((tm, d), BF16), pltpu.VMEM((tm, d), F32)],
        compiler_params=_params("parallel", "arbitrary"),
        name="ffn",
    )(x, g, wg, wu, wd)


INFO_LANES = 128


def _router_kernel(x_ref, g_ref, whi_ref, wlo_ref, info_ref, cnt_ref, *, n_exp):
    @pl.when(pl.program_id(0) == 0)
    def _():
        cnt_ref[...] = jnp.zeros_like(cnt_ref)

    h = _rms(x_ref[...], g_ref[...])
    hi, lo = _split_bf16(h)
    logits = _dot(hi, whi_ref[...]) + (_dot(lo, whi_ref[...]) + _dot(hi, wlo_ref[...]))
    tm = logits.shape[0]
    lane = lax.broadcasted_iota(jnp.int32, logits.shape, 1).astype(F32)
    neg = jnp.float32(-jnp.inf)
    big = jnp.float32(logits.shape[1])
    l1 = jnp.where(lane < n_exp, logits, neg)
    m1 = jnp.max(l1, axis=1, keepdims=True)
    i1 = jnp.min(jnp.where(l1 == m1, lane, big), axis=1, keepdims=True)
    l2 = jnp.where(lane == i1, neg, l1)
    m2 = jnp.max(l2, axis=1, keepdims=True)
    i2 = jnp.min(jnp.where(l2 == m2, lane, big), axis=1, keepdims=True)
    e2 = jnp.exp(m2 - m1)
    den = 1.0 + e2
    sel = jnp.where(lane == i1, 1.0, 0.0) + jnp.where(lane == i2, 1.0, 0.0)
    r = lax.broadcasted_iota(jnp.int32, (tm, tm), 0)
    c = lax.broadcasted_iota(jnp.int32, (tm, tm), 1)
    earlier_rows = jnp.where(r > c, 1.0, 0.0).astype(BF16)
    before = _dot(earlier_rows, sel.astype(BF16)) + cnt_ref[...]
    r1 = jnp.sum(jnp.where(lane == i1, before, 0.0), axis=1, keepdims=True)
    r2 = jnp.sum(jnp.where(lane == i2, before, 0.0), axis=1, keepdims=True)
    fields = (i1, i2, 1.0 / den, e2 / den, r1, r2)
    info = jnp.zeros_like(logits)
    for k, f in enumerate(fields):
        info = jnp.where(lane == k, f, info)
    info_ref[...] = info
    cnt_ref[...] += jnp.sum(sel, axis=0, keepdims=True)


def _router(x, g, w_hi, w_lo, n_exp, tm):
    m, d = x.shape
    tm = _row_tile(m, tm)
    row = lambda i: (i, 0)
    fix = lambda i: (0, 0)
    return pl.pallas_call(
        functools.partial(_router_kernel, n_exp=n_exp),
        grid=(m // tm,),
        in_specs=[pl.BlockSpec((tm, d), row), pl.BlockSpec((1, d), fix),
                  pl.BlockSpec((d, INFO_LANES), fix), pl.BlockSpec((d, INFO_LANES), fix)],
        out_specs=[pl.BlockSpec((tm, INFO_LANES), row), pl.BlockSpec((1, INFO_LANES), fix)],
        out_shape=[jax.ShapeDtypeStruct((m, INFO_LANES), F32), jax.ShapeDtypeStruct((1, INFO_LANES), F32)],
        compiler_params=_params("arbitrary"),
        name="router",
    )(x, g, w_hi, w_lo)


def _routing_plan(info, cnt, n_exp, tg, n_tiles):
    i1, i2 = info[:, 0].astype(jnp.int32), info[:, 1].astype(jnp.int32)
    r1, r2 = info[:, 4].astype(jnp.int32), info[:, 5].astype(jnp.int32)
    counts = cnt[0, :n_exp].astype(jnp.int32)
    padded = (counts + tg - 1) // tg * tg
    ends = jnp.cumsum(padded)
    starts = ends - padded
    pos1 = jnp.take(starts, i1) + r1
    pos2 = jnp.take(starts, i2) + r2
    n_active = ends[n_exp - 1] // tg
    tiles = jnp.arange(n_tiles, dtype=jnp.int32)
    tile_expert = jnp.sum((tiles[:, None] * tg >= ends[None, :]).astype(jnp.int32), axis=1)
    tile_expert = jnp.minimum(tile_expert, jnp.take(tile_expert, jnp.maximum(n_active - 1, 0)))
    return pos1, pos2, tile_expert, n_active.reshape(1)


def _row_copies(copies_of, sem, n_rows):
    def start(r, carry):
        for src, dst in copies_of(r):
            pltpu.make_async_copy(src, dst, sem).start()
        return carry

    def wait(r, carry):
        for src, dst in copies_of(r):
            pltpu.make_async_copy(src, dst, sem).wait()
        return carry

    lax.fori_loop(0, n_rows, start, 0)
    lax.fori_loop(0, n_rows, wait, 0)


def _dispatch_kernel(pos1_ref, pos2_ref, x_ref, g_ref, init_ref, xs_ref, h_ref, sem):
    del init_ref
    tm = x_ref.shape[0]
    base = pl.program_id(0) * tm
    h_ref[...] = _rms(x_ref[...], g_ref[...])

    def copies_of(r):
        src = h_ref.at[pl.ds(r, 1), :]
        return [(src, xs_ref.at[pl.ds(pos_ref[base + r], 1), :]) for pos_ref in (pos1_ref, pos2_ref)]

    _row_copies(copies_of, sem, tm)


def _dispatch(x, g, pos1, pos2, n_slots, tm):
    m, d = x.shape
    tm = _row_tile(m, tm)
    grid_spec = pltpu.PrefetchScalarGridSpec(
        num_scalar_prefetch=2,
        grid=(m // tm,),
        in_specs=[pl.BlockSpec((tm, d), lambda i, p1, p2: (i, 0)), pl.BlockSpec((1, d), lambda i, p1, p2: (0, 0)),
                  pl.BlockSpec(memory_space=pl.ANY)],
        out_specs=pl.BlockSpec(memory_space=pl.ANY),
        scratch_shapes=[pltpu.VMEM((tm, d), F32), pltpu.SemaphoreType.DMA(())],
    )
    return pl.pallas_call(
        _dispatch_kernel,
        grid_spec=grid_spec,
        out_shape=jax.ShapeDtypeStruct((n_slots, d), F32),
        input_output_aliases={4: 0},
        compiler_params=_params("arbitrary"),
        name="moe_dispatch",
    )(pos1, pos2, x, g, jnp.zeros((n_slots, d), F32))


def _experts_kernel(te_ref, na_ref, x_ref, wg_ref, wu_ref, wd_ref, o_ref, h_ref, acc_ref):
    del te_ref
    i = pl.program_id(0)
    j = pl.program_id(1)
    last = pl.num_programs(1) - 1
    active = i < na_ref[0]

    @pl.when(active & (j == 0))
    def _():
        h_ref[...] = x_ref[...].astype(BF16)
        acc_ref[...] = jnp.zeros_like(acc_ref)

    @pl.when(active)
    def _():
        acc_ref[...] += _swiglu_step(h_ref[...], wg_ref[0], wu_ref[0], wd_ref[0])

    @pl.when(active & (j == last))
    def _():
        o_ref[...] = acc_ref[...]

    @pl.when(jnp.logical_not(active) & (j == last))
    def _():
        o_ref[...] = jnp.zeros_like(o_ref)


def _experts(xs, tile_expert, n_active, wg, wu, wd, tg, tf):
    n_slots, d = xs.shape
    f = wg.shape[2]
    tf = _row_tile(f, tf)
    nj = f // tf

    def col(i, j, te, na):
        return jnp.where(i < na[0], j, nj - 1)

    grid_spec = pltpu.PrefetchScalarGridSpec(
        num_scalar_prefetch=2,
        grid=(n_slots // tg, nj),
        in_specs=[pl.BlockSpec((tg, d), lambda i, j, te, na: (i, 0)),
                  pl.BlockSpec((1, d, tf), lambda i, j, te, na: (te[i], 0, col(i, j, te, na))),
                  pl.BlockSpec((1, d, tf), lambda i, j, te, na: (te[i], 0, col(i, j, te, na))),
                  pl.BlockSpec((1, tf, d), lambda i, j, te, na: (te[i], col(i, j, te, na), 0))],
        out_specs=pl.BlockSpec((tg, d), lambda i, j, te, na: (i, 0)),
        scratch_shapes=[pltpu.VMEM((tg, d), BF16), pltpu.VMEM((tg, d), F32)],
    )
    return pl.pallas_call(
        _experts_kernel,
        grid_spec=grid_spec,
        out_shape=jax.ShapeDtypeStruct((n_slots, d), F32),
        compiler_params=_params("parallel", "arbitrary"),
        name="moe_experts",
    )(tile_expert, n_active, xs, wg, wu, wd)


def _combine_kernel(pos1_ref, pos2_ref, x_ref, info_ref, ys_ref, o_ref, buf_ref, sem):
    tm = x_ref.shape[0]
    base = pl.program_id(0) * tm

    def copies_of(r):
        return [(ys_ref.at[pl.ds(pos_ref[base + r], 1), :], buf_ref.at[k, pl.ds(r, 1), :])
                for k, pos_ref in enumerate((pos1_ref, pos2_ref))]

    _row_copies(copies_of, sem, tm)
    o_ref[...] = x_ref[...] + info_ref[:, 2:3] * buf_ref[0] + info_ref[:, 3:4] * buf_ref[1]


def _combine(x, info, ys, pos1, pos2, tm):
    m, d = x.shape
    tm = _row_tile(m, tm)
    row = lambda i, p1, p2: (i, 0)
    grid_spec = pltpu.PrefetchScalarGridSpec(
        num_scalar_prefetch=2,
        grid=(m // tm,),
        in_specs=[pl.BlockSpec((tm, d), row), pl.BlockSpec((tm, INFO_LANES), row),
                  pl.BlockSpec(memory_space=pl.ANY)],
        out_specs=pl.BlockSpec((tm, d), row),
        scratch_shapes=[pltpu.VMEM((2, tm, d), F32), pltpu.SemaphoreType.DMA(())],
    )
    return pl.pallas_call(
        _combine_kernel,
        grid_spec=grid_spec,
        out_shape=jax.ShapeDtypeStruct((m, d), F32),
        compiler_params=_params("arbitrary"),
        name="moe_combine",
    )(pos1, pos2, x, info, ys)


def _moe(x, g, w_hi, w_lo, wg, wu, wd, n_exp, tm, tg, tf):
    m = x.shape[0]
    tg = min(tg, _row_tile(m, tg) * TOP_K)
    n_tiles = -(-(TOP_K * m) // tg) + n_exp
    info, cnt = _router(x, g, w_hi, w_lo, n_exp, tm)
    pos1, pos2, tile_expert, n_active = _routing_plan(info, cnt, n_exp, tg, n_tiles)
    xs = _dispatch(x, g, pos1, pos2, n_tiles * tg, tm)
    ys = _experts(xs, tile_expert, n_active, wg, wu, wd, tg, tf)
    return _combine(x, info, ys, pos1, pos2, tm)


def _norm_proj_kernel(x_ref, g_ref, w_ref, o_ref, h_ref):
    @pl.when(pl.program_id(1) == 0)
    def _():
        h_ref[...] = _rms(x_ref[...], g_ref[...]).astype(BF16)

    o_ref[...] = _dot(h_ref[...], w_ref[...])


def _norm_proj(x, g, w_bf, tm, tn):
    m, d = x.shape
    n = w_bf.shape[1]
    tm = _row_tile(m, tm)
    tn = _row_tile(n, tn)
    return pl.pallas_call(
        _norm_proj_kernel,
        grid=(m // tm, n // tn),
        in_specs=[pl.BlockSpec((tm, d), lambda i, j: (i, 0)), pl.BlockSpec((1, d), lambda i, j: (0, 0)),
                  pl.BlockSpec((d, tn), lambda i, j: (0, j))],
        out_specs=pl.BlockSpec((tm, tn), lambda i, j: (i, j)),
        out_shape=jax.ShapeDtypeStruct((m, n), F32),
        scratch_shapes=[pltpu.VMEM((tm, d), BF16)],
        compiler_params=_params("parallel", "arbitrary"),
        name="norm_proj",
    )(x, g, w_bf)


def _rotate(x, cos, sin_signed):
    return x * cos + pltpu.roll(x, x.shape[1] // 2, axis=1) * sin_signed


def _group_norm_gate(o, gate, gn_g, gn_b):
    mu = jnp.mean(o, axis=-1, keepdims=True)
    var = jnp.mean(jnp.square(o - mu), axis=-1, keepdims=True)
    on = (o - mu) * lax.rsqrt(var + EPS) * gn_g + gn_b
    return gate * jax.nn.sigmoid(gate) * on


def _ret_prompt_kernel(gc_ref, q_ref, k_ref, v_ref, gate_ref, cos_ref, sin_ref, dec_ref, qdec_ref, kdec_ref,
                       gng_ref, gnb_ref, y_ref, s_ref, *, heads, dk, dv):
    @pl.when(pl.program_id(1) == 0)
    def _():
        s_ref[...] = jnp.zeros_like(s_ref)

    cos = cos_ref[...]
    sin = sin_ref[...]
    for h in range(heads):
        kl = slice(h * dk, (h + 1) * dk)
        vl = slice(h * dv, (h + 1) * dv)
        qr = _rotate(q_ref[:, kl], cos, sin)
        kr = _rotate(k_ref[:, kl], cos, sin) * (dk ** -0.5)
        qb = qr.astype(BF16)
        vb = v_ref[:, vl].astype(BF16)
        state = s_ref[0, h]
        scores = _dot_nt(qb, kr.astype(BF16)) * dec_ref[h]
        o = _dot(scores.astype(BF16), vb) + _dot(qb, state.astype(BF16)) * qdec_ref[h]
        kd_t = (kr * kdec_ref[h]).T.astype(BF16)
        s_ref[0, h] = gc_ref[h] * state + _dot(kd_t, vb)
        y_ref[:, vl] = _group_norm_gate(o, gate_ref[:, vl], gng_ref[:, vl], gnb_ref[:, vl])


def _ret_prompt(proj, bsz, t, gc, cos, sin, dec, qdec, kdec, gn_g, gn_b, heads, dk, dv):
    m = proj.shape[0]
    c = dec.shape[1]
    nc = t // c
    qk, vw = heads * dk, heads * dv
    assert vw == 2 * qk
    tok = lambda col: (lambda b, ci: (b * nc + ci, col))
    fix2 = lambda b, ci: (0, 0)
    fix3 = lambda b, ci: (0, 0, 0)
    return pl.pallas_call(
        functools.partial(_ret_prompt_kernel, heads=heads, dk=dk, dv=dv),
        grid=(bsz, nc),
        in_specs=[pl.BlockSpec(memory_space=pltpu.SMEM),
                  pl.BlockSpec((c, qk), tok(0)), pl.BlockSpec((c, qk), tok(1)),
                  pl.BlockSpec((c, vw), tok(1)), pl.BlockSpec((c, vw), tok(2)),
                  pl.BlockSpec((c, dk), lambda b, ci: (ci, 0)), pl.BlockSpec((c, dk), lambda b, ci: (ci, 0)),
                  pl.BlockSpec(dec.shape, fix3), pl.BlockSpec(qdec.shape, fix3), pl.BlockSpec(kdec.shape, fix3),
                  pl.BlockSpec((1, vw), fix2), pl.BlockSpec((1, vw), fix2)],
        out_specs=[pl.BlockSpec((c, vw), lambda b, ci: (b * nc + ci, 0)),
                   pl.BlockSpec((1, heads, dk, dv), lambda b, ci: (b, 0, 0, 0))],
        out_shape=[jax.ShapeDtypeStruct((m, vw), F32), jax.ShapeDtypeStruct((bsz, heads, dk, dv), F32)],
        compiler_params=_params("parallel", "arbitrary"),
        name="ret_prompt",
    )(gc, proj, proj, proj, proj, cos, sin, dec, qdec, kdec, gn_g, gn_b)


def _ret_sample_kernel(gamma_ref, p_ref, s_ref, cos_ref, sin_ref, gng_ref, gnb_ref, y_ref, so_ref,
                       *, heads, dk, dv):
    qk, vw = heads * dk, heads * dv
    cos = cos_ref[...]
    sin = sin_ref[...]

    def column(x):
        col = jnp.broadcast_to(x, (dk, dk)).T
        return jnp.concatenate([col] * (dv // dk), axis=1)

    for h in range(heads):
        q = _rotate(p_ref[0, :, h * dk:(h + 1) * dk], cos, sin)
        k = _rotate(p_ref[0, :, qk + h * dk:qk + (h + 1) * dk], cos, sin) * (dk ** -0.5)
        vl = slice(h * dv, (h + 1) * dv)
        v = p_ref[0, :, 2 * qk + h * dv:2 * qk + (h + 1) * dv]
        gate = p_ref[0, :, 2 * qk + vw + h * dv:2 * qk + vw + (h + 1) * dv]
        state = s_ref[0, h]
        o = jnp.sum(q * k, axis=-1, keepdims=True) * v \
            + jnp.sum(column(q) * state, axis=0, keepdims=True) * gamma_ref[h]
        so_ref[0, h] = gamma_ref[h] * state + column(k) * v
        y_ref[0, :, vl] = _group_norm_gate(o, gate, gng_ref[:, vl], gnb_ref[:, vl])


def _ret_sample(proj, state, gamma, cos, sin, gn_g, gn_b, heads, dk, dv):
    s, n = proj.shape
    vw = heads * dv
    p3 = proj.reshape(s, 1, n)
    fix2 = lambda i: (0, 0)
    y, new_state = pl.pallas_call(
        functools.partial(_ret_sample_kernel, heads=heads, dk=dk, dv=dv),
        grid=(s,),
        in_specs=[pl.BlockSpec(memory_space=pltpu.SMEM),
                  pl.BlockSpec((1, 1, n), lambda i: (i, 0, 0)),
                  pl.BlockSpec((1, heads, dk, dv), lambda i: (i, 0, 0, 0)),
                  pl.BlockSpec((1, dk), fix2), pl.BlockSpec((1, dk), fix2),
                  pl.BlockSpec((1, vw), fix2), pl.BlockSpec((1, vw), fix2)],
        out_specs=[pl.BlockSpec((1, 1, vw), lambda i: (i, 0, 0)),
                   pl.BlockSpec((1, heads, dk, dv), lambda i: (i, 0, 0, 0))],
        out_shape=[jax.ShapeDtypeStruct((s, 1, vw), F32), jax.ShapeDtypeStruct(state.shape, F32)],
        compiler_params=_params("parallel"),
        name="ret_sample",
    )(gamma, p3, state, cos, sin, gn_g, gn_b)
    return y.reshape(s, vw), new_state


def _rope_tables(pos, dk):
    half = dk // 2
    inv_freq = ROPE_BASE ** (-jnp.linspace(0.0, 1.0, half, dtype=F32))
    ang = pos.astype(F32)[:, None] * inv_freq[None, :]
    cos, sin = jnp.cos(ang), jnp.sin(ang)
    return jnp.concatenate([cos, cos], axis=1), jnp.concatenate([-sin, sin], axis=1)


def _decay_tables(c, dk, dv):
    log_g = jnp.log1p(-jnp.exp2(-5.0 - jnp.arange(RET_HEADS, dtype=F32)))
    idx = jnp.arange(c, dtype=F32)
    diff = idx[:, None] - idx[None, :]
    dec = jnp.where(diff >= 0, jnp.exp(log_g[:, None, None] * jnp.maximum(diff, 0.0)), 0.0)
    qdec = jnp.exp(log_g[:, None] * (idx[None, :] + 1.0))
    kdec = jnp.exp(log_g[:, None] * (c - 1.0 - idx[None, :]))
    qdec = jnp.broadcast_to(qdec[:, :, None], (RET_HEADS, c, dv))
    kdec = jnp.broadcast_to(kdec[:, :, None], (RET_HEADS, c, dk))
    return dec, qdec, kdec, jnp.exp(log_g * c), jnp.exp(log_g)


def kernel(x_prompt, x_sample, cache_sb_k, cache_sb_v, cache_conv, state_ret, page_table, norm_mix_even, w_in_even, conv_w, conv_b, conv_ln_g, conv_ln_b, sb_q_norm, sb_k_norm, sb_bias, w_out_even, norm_ffn_even, ffn_w_gate, ffn_w_up, ffn_w_down, norm_mix_odd, w_in_odd, ret_gn_g, ret_gn_b, w_out_odd, norm_ffn_odd, router_w, moe_w_gate, moe_w_up, moe_w_down):
    bsz, t, d = x_prompt.shape
    s = x_sample.shape[0]
    assert x_sample.shape[1] == 1
    n_layers = w_in_even.shape[0] + w_in_odd.shape[0]
    heads, dh = sb_bias.shape[1], sb_q_norm.shape[1]
    w = heads * dh
    c = conv_w.shape[2]
    cw = conv_w.shape[1]
    page = cache_sb_k.shape[2]
    n_ctx = page_table.shape[1] * page
    dk = d // RET_HEADS
    dv = 2 * dk
    n_exp = router_w.shape[2]

    yp = x_prompt.reshape(bsz * t, d)
    ys = x_sample.reshape(s, d)
    row = lambda a: a.reshape(1, -1)

    head_mean = jnp.kron(jnp.eye(heads, dtype=F32), jnp.full((dh, dh), 1.0 / dh, F32)).astype(BF16)
    cos_p, sin_p = _rope_tables(jnp.arange(t, dtype=jnp.int32), dk)
    cos_s, sin_s = _rope_tables(jnp.full((1,), n_ctx, jnp.int32), dk)
    chunk = RET_CHUNK if t % RET_CHUNK == 0 else t
    dec, qdec, kdec, gamma_chunk, gamma = _decay_tables(chunk, dk, dv)

    kp_l, vp_l, ks_l, vs_l, cp_l, cs_l, rp_l, rs_l = [], [], [], [], [], [], [], []
    for layer in range(n_layers):
        li = layer // 2
        if layer % 2 == 0:
            w_in = w_in_even[li].astype(BF16)
            qg = row(jnp.tile(sb_q_norm[li], heads))
            kg = row(jnp.tile(sb_k_norm[li], heads))
            w_out = w_out_even[li].astype(BF16)
            conv_args = (conv_w[li], row(conv_b[li]), row(conv_ln_g[li]), row(conv_ln_b[li]))
            g_mix = row(norm_mix_even[li])

            up, kp, vp, qpb, kpb, vpb = _even_in(yp, g_mix, w_in, qg, kg, head_mean, c, w, dh, 512)
            us, k_s, v_s, qsb, _, _ = _even_in(ys, g_mix, w_in, qg, kg, head_mean, c, w, dh, 128)

            up3 = up.reshape(bsz, t, c)
            full = jnp.pad(up3, ((0, 0), (CONV_HEAD, 0), (0, 0)))
            cp = _conv_prompt(full, *conv_args, t, 256).reshape(bsz * t, c)
            cs = _conv_sample(cache_conv[li], us, *conv_args)

            op = _sb_prompt(sb_bias[li], qpb.reshape(bsz, t, w), kpb.reshape(bsz, t, w),
                            vpb.reshape(bsz, t, w), heads, dh, 256).reshape(bsz * t, w)
            n_pool = cache_sb_k.shape[1]
            n_pages = page_table.shape[1]
            pool_kt = cache_sb_k[li].transpose(0, 2, 3, 1).reshape(n_pool, w, page)
            pool_vt = cache_sb_v[li].transpose(0, 2, 3, 1).reshape(n_pool, w, page)
            os_ = _sb_decode(page_table, qsb, jnp.tile(sb_bias[li], n_pages).reshape(n_pages * heads, 1),
                             pool_kt, pool_vt, heads, dh)

            yp = _out_proj(yp, [cp, op], [w_out[:c], w_out[c:]], 1024)
            ys = _out_proj(ys, [cs, os_], [w_out[:c], w_out[c:]], 128)

            ffn_w = (ffn_w_gate[li].astype(BF16), ffn_w_up[li].astype(BF16), ffn_w_down[li].astype(BF16))
            g_ffn = row(norm_ffn_even[li])
            yp = _ffn(yp, g_ffn, *ffn_w, 1024, 256)
            ys = _ffn(ys, g_ffn, *ffn_w, 128, 256)

            kp_l.append(kp.reshape(bsz, t, heads, dh))
            vp_l.append(vp.reshape(bsz, t, heads, dh))
            ks_l.append(k_s.reshape(s, 1, heads, dh))
            vs_l.append(v_s.reshape(s, 1, heads, dh))
            cp_l.append(up3[:, t - (cw - 1):])
            cs_l.append(jnp.concatenate([cache_conv[li][:, 1:], us[:, None, :]], axis=1))
        else:
            w_in = w_in_odd[li].astype(BF16)
            w_out = w_out_odd[li].astype(BF16)
            g_mix = row(norm_mix_odd[li])
            gn_g, gn_b = row(ret_gn_g[li]), row(ret_gn_b[li])

            pp = _norm_proj(yp, g_mix, w_in, 1024, 2048)
            ps = _norm_proj(ys, g_mix, w_in, 128, 2048)
            mp, sp = _ret_prompt(pp, bsz, t, gamma_chunk, cos_p, sin_p, dec, qdec, kdec, gn_g, gn_b,
                                 RET_HEADS, dk, dv)
            ms, ss = _ret_sample(ps, state_ret[li], gamma, cos_s, sin_s, gn_g, gn_b, RET_HEADS, dk, dv)
            yp = _out_proj(yp, [mp], [w_out], 1024)
            ys = _out_proj(ys, [ms], [w_out], 128)

            g_ffn = row(norm_ffn_odd[li])
            rw = jnp.pad(router_w[li], ((0, 0), (0, INFO_LANES - n_exp)))
            rw_hi = rw.astype(BF16)
            rw_lo = (rw - rw_hi.astype(F32)).astype(BF16)
            moe_w = (moe_w_gate[li].astype(BF16), moe_w_up[li].astype(BF16), moe_w_down[li].astype(BF16))
            yp = _moe(yp, g_ffn, rw_hi, rw_lo, *moe_w, n_exp, 512, 512, 512)
            ys = _moe(ys, g_ffn, rw_hi, rw_lo, *moe_w, n_exp, 128, 512, 512)
            rp_l.append(sp)
            rs_l.append(ss)

    return (yp.reshape(bsz, t, d), ys.reshape(s, 1, d), jnp.stack(kp_l), jnp.stack(vp_l), jnp.stack(ks_l),
            jnp.stack(vs_l), jnp.stack(cp_l), jnp.stack(cs_l), jnp.stack(rp_l), jnp.stack(rs_l))
```

```python
import functools

import jax
import jax.numpy as jnp
from jax import lax
from jax.experimental import pallas as pl
from jax.experimental.pallas import tpu as pltpu

F32 = jnp.float32
BF16 = jnp.bfloat16

EPS = 1e-6
RET_HEADS = 8
RET_CHUNK = 128
ROPE_BASE = 10000.0
TOP_K = 2

V7X_VMEM_LIMIT_BYTES = 56 * 1024 * 1024


def _params(*sem):
    return pltpu.CompilerParams(dimension_semantics=sem, vmem_limit_bytes=V7X_VMEM_LIMIT_BYTES)


def _row_tile(m, want):
    t = min(m, want)
    while m % t:
        t //= 2
    return t


def _rms(x, g):
    return x * lax.rsqrt(jnp.mean(x * x, axis=-1, keepdims=True) + EPS) * g


def _split_bf16(x):
    hi = x.astype(BF16)
    lo = (x - hi.astype(F32)).astype(BF16)
    return hi, lo


LOG2_E = 1.4426950408889634


def _log2_sigmoid(z2):
    return jnp.minimum(z2, 0.0) - jnp.log2(1.0 + jnp.exp2(-jnp.abs(z2)))


def _dot(a, b):
    return jnp.dot(a, b, preferred_element_type=F32)


def _dot_nt(a, b):
    return lax.dot_general(a, b, (((1,), (1,)), ((), ())), preferred_element_type=F32)


def _even_in_kernel(x_ref, g_ref, w_ref, qg_ref, kg_ref, hm_ref,
                    u_ref, k_ref, v_ref, qb_ref, kb_ref, vb_ref, *, c, w, qscale):
    hb = _rms(x_ref[...], g_ref[...]).astype(BF16)

    def proj(lo, width):
        return _dot(hb, w_ref[:, lo:lo + width])

    def head_norm(t, gain):
        hi, lo = _split_bf16(t * t)
        ms = _dot(hi, hm_ref[...]) + _dot(lo, hm_ref[...])
        return t * lax.rsqrt(ms + EPS) * gain

    a = proj(0, c)
    gate = proj(c, c)
    u_ref[...] = a * jax.nn.sigmoid(gate)
    q = head_norm(proj(2 * c, w), qg_ref[...])
    qb_ref[...] = (q * qscale).astype(BF16)
    k = head_norm(proj(2 * c + w, w), kg_ref[...])
    k_ref[...] = k
    kb_ref[...] = k.astype(BF16)
    v = proj(2 * c + 2 * w, w)
    v_ref[...] = v
    vb_ref[...] = v.astype(BF16)


def _even_in(x, g, w_bf, qg_t, kg_t, hm, c, w, dh, tm):
    m, d = x.shape
    n = w_bf.shape[1]
    tm = _row_tile(m, tm)
    row = lambda i: (i, 0)
    fix = lambda i: (0, 0)
    out_f = jax.ShapeDtypeStruct((m, w), F32)
    out_b = jax.ShapeDtypeStruct((m, w), BF16)
    return pl.pallas_call(
        functools.partial(_even_in_kernel, c=c, w=w, qscale=dh ** -0.5 * LOG2_E),
        grid=(m // tm,),
        in_specs=[pl.BlockSpec((tm, d), row), pl.BlockSpec((1, d), fix), pl.BlockSpec((d, n), fix),
                  pl.BlockSpec((1, w), fix), pl.BlockSpec((1, w), fix), pl.BlockSpec((w, w), fix)],
        out_specs=[pl.BlockSpec((tm, c), row)] + [pl.BlockSpec((tm, w), row)] * 5,
        out_shape=[jax.ShapeDtypeStruct((m, c), F32), out_f, out_f, out_b, out_b, out_b],
        compiler_params=_params("parallel"),
        name="even_in",
    )(x, g, w_bf, qg_t, kg_t, hm)


CONV_HEAD = 32


def _ln_silu(cv, g, b):
    mu = jnp.mean(cv, axis=-1, keepdims=True)
    var = jnp.mean(jnp.square(cv - mu), axis=-1, keepdims=True)
    y = (cv - mu) * lax.rsqrt(var + EPS) * g + b
    return y * jax.nn.sigmoid(y)


def _conv_prompt_kernel(full_ref, w_ref, b_ref, g_ref, beta_ref, o_ref, *, tt, cw):
    t = pl.program_id(1)
    n = tt + CONV_HEAD
    win = full_ref[0, pl.ds(pl.multiple_of(t * tt, tt), n), :]
    acc = jnp.zeros((tt, win.shape[1]), F32)
    for j in range(cw):
        off = j + CONV_HEAD - (cw - 1)
        shifted = win if off == 0 else pltpu.roll(win, n - off, axis=0)
        acc = acc + shifted[:tt] * w_ref[j:j + 1, :]
    o_ref[0] = _ln_silu(acc + b_ref[...], g_ref[...], beta_ref[...])


def _conv_prompt(full, w, b, g, beta, t_len, tt):
    bsz, tf, c = full.shape
    cw = w.shape[0]
    tt = _row_tile(t_len, tt)
    fix = lambda i, t: (0, 0)
    return pl.pallas_call(
        functools.partial(_conv_prompt_kernel, tt=tt, cw=cw),
        grid=(bsz, t_len // tt),
        in_specs=[pl.BlockSpec((1, tf, c), lambda i, t: (i, 0, 0)), pl.BlockSpec((cw, c), fix),
                  pl.BlockSpec((1, c), fix), pl.BlockSpec((1, c), fix), pl.BlockSpec((1, c), fix)],
        out_specs=pl.BlockSpec((1, tt, c), lambda i, t: (i, t, 0)),
        out_shape=jax.ShapeDtypeStruct((bsz, t_len, c), F32),
        compiler_params=_params("parallel", "arbitrary"),
        name="conv_prompt",
    )(full, w, b, g, beta)


def _conv_sample_kernel(buf_ref, u_ref, w_ref, b_ref, g_ref, beta_ref, o_ref, *, cw):
    past = jnp.sum(buf_ref[...] * w_ref[0:cw - 1, :][None], axis=1)
    cv = past + u_ref[...] * w_ref[cw - 1:cw, :] + b_ref[...]
    o_ref[...] = _ln_silu(cv, g_ref[...], beta_ref[...])


def _conv_sample(buf, u, w, b, g, beta):
    s, nb, c = buf.shape
    cw = w.shape[0]
    ts = _row_tile(s, 8)
    fix = lambda i: (0, 0)
    return pl.pallas_call(
        functools.partial(_conv_sample_kernel, cw=cw),
        grid=(s // ts,),
        in_specs=[pl.BlockSpec((ts, nb, c), lambda i: (i, 0, 0)), pl.BlockSpec((ts, c), lambda i: (i, 0)),
                  pl.BlockSpec((cw, c), fix), pl.BlockSpec((1, c), fix), pl.BlockSpec((1, c), fix),
                  pl.BlockSpec((1, c), fix)],
        out_specs=pl.BlockSpec((ts, c), lambda i: (i, 0)),
        out_shape=jax.ShapeDtypeStruct((s, c), F32),
        compiler_params=_params("parallel"),
        name="conv_sample",
    )(buf, u, w, b, g, beta)


def _strict_lower_ones(n, repeat=1):
    r = lax.broadcasted_iota(jnp.int32, (repeat * n, n), 0) % n
    c = lax.broadcasted_iota(jnp.int32, (repeat * n, n), 1)
    return jnp.where(r > c, 1.0, 0.0).astype(BF16)


def _suffix_after(lk, ones_after2):
    return _dot(jnp.concatenate(_split_bf16(lk), axis=1), ones_after2)


def _sb_blocks(qs, k_blks, v_blks, biases2, runs, ones_after2, mask):
    n = len(qs)
    z2 = [_dot_nt(qs[g], k_blks[g]) + biases2[g] for g in range(n)]
    ls = [_log2_sigmoid(z) for z in z2]
    lk = [l - z for l, z in zip(ls, z2)]
    if mask is not None:
        lk = [jnp.where(mask, x, 0.0) for x in lk]
    suf = [_suffix_after(x, ones_after2) for x in lk]
    a = [jnp.exp2(ls[g] + suf[g] + runs[g]) for g in range(n)]
    if mask is not None:
        a = [jnp.where(mask, x, 0.0) for x in a]
    pv = [_dot(a[g].astype(BF16), v_blks[g]) for g in range(n)]
    return pv, [runs[g] + jnp.sum(lk[g], axis=1, keepdims=True) for g in range(n)]


def _sb_prompt_kernel(bias_ref, q_ref, k_ref, v_ref, o_ref, *, tq, heads, dh, group):
    i = pl.program_id(1)
    row = lax.broadcasted_iota(jnp.int32, (tq, tq), 0)
    col = lax.broadcasted_iota(jnp.int32, (tq, tq), 1)
    causal = row > col
    ones_after = _strict_lower_ones(tq, repeat=2)
    for h0 in range(0, heads, group):
        lanes = [slice(h * dh, (h + 1) * dh) for h in range(h0, h0 + group)]
        qs = [q_ref[0, :, l] for l in lanes]
        biases = [bias_ref[h] * LOG2_E for h in range(h0, h0 + group)]

        def tile(j, carry, mask, lanes=lanes, qs=qs, biases=biases):
            rows = pl.ds(pl.multiple_of(j * tq, tq), tq)
            pv, runs = _sb_blocks(qs, [k_ref[0, rows, l] for l in lanes], [v_ref[0, rows, l] for l in lanes],
                                  biases, [c[1] for c in carry], ones_after, mask)
            return tuple((carry[g][0] + pv[g], runs[g]) for g in range(group))

        init = tuple((jnp.zeros((tq, dh), F32), jnp.zeros((tq, 1), F32)) for _ in range(group))
        carry = tile(i, init, causal)
        carry = lax.fori_loop(1, i + 1, lambda jj, c, tile=tile: tile(i - jj, c, None), carry)
        for g in range(group):
            o_ref[0, :, lanes[g]] = carry[g][0]


def _sb_prompt(bias, qb, kb, vb, heads, dh, tq):
    bsz, t, w = qb.shape
    tq = _row_tile(t, tq)
    seq = lambda b, i: (b, 0, 0)
    return pl.pallas_call(
        functools.partial(_sb_prompt_kernel, tq=tq, heads=heads, dh=dh, group=4),
        grid=(bsz, t // tq),
        in_specs=[pl.BlockSpec(memory_space=pltpu.SMEM),
                  pl.BlockSpec((1, tq, w), lambda b, i: (b, i, 0)),
                  pl.BlockSpec((1, t, w), seq), pl.BlockSpec((1, t, w), seq)],
        out_specs=pl.BlockSpec((1, tq, w), lambda b, i: (b, i, 0)),
        out_shape=jax.ShapeDtypeStruct((bsz, t, w), F32),
        compiler_params=_params("parallel", "arbitrary"),
        name="sb_prompt",
    )(bias, qb, kb, vb)


def _sb_decode_kernel(pt_ref, q_ref, bias_ref, *refs, n_pages, heads, dh):
    del pt_ref
    k_refs = refs[:n_pages]
    v_refs = refs[n_pages:2 * n_pages]
    o_ref = refs[2 * n_pages]
    w = heads * dh
    page = k_refs[0].shape[2]
    rows = n_pages * heads
    head_of_lane = lax.broadcasted_iota(jnp.int32, (heads, w), 1) // dh
    head_mask = head_of_lane == lax.broadcasted_iota(jnp.int32, (heads, w), 0)
    q_bd = jnp.where(head_mask, jnp.broadcast_to(q_ref[0].astype(F32), (heads, w)), 0.0).astype(BF16)
    z = jnp.concatenate([_dot(q_bd, k_refs[p][0].astype(BF16)) for p in range(n_pages)], axis=0)
    z2 = z + bias_ref[...] * LOG2_E
    ls = _log2_sigmoid(z2)
    lk = ls - z2
    hi, lo = _split_bf16(lk)
    ones_after = _strict_lower_ones(page)
    within = _dot(hi, ones_after) + _dot(lo, ones_after)
    all_ones = jnp.ones((page, page), BF16)
    t_hi, t_lo = _split_bf16(_dot(hi, all_ones) + _dot(lo, all_ones))
    r = lax.broadcasted_iota(jnp.int32, (rows, rows), 0)
    c = lax.broadcasted_iota(jnp.int32, (rows, rows), 1)
    later_page = jnp.where((r % heads == c % heads) & (c // heads > r // heads), 1.0, 0.0).astype(BF16)
    run = _dot(later_page, t_hi) + _dot(later_page, t_lo)
    a = jnp.exp2(ls + within + run).astype(BF16)
    acc = jnp.zeros((heads, w), F32)
    for p in range(n_pages):
        acc = acc + _dot_nt(a[p * heads:(p + 1) * heads], v_refs[p][0].astype(BF16))
    o_ref[0] = jnp.sum(jnp.where(head_mask, acc, 0.0), axis=0, keepdims=True)


def _sb_decode(page_table, qb, bias_rows, pool_kt, pool_vt, heads, dh):
    s, n_pages = page_table.shape
    _, w, page = pool_kt.shape
    q3 = qb.reshape(s, 1, w)

    def page_spec(p):
        return pl.BlockSpec((1, w, page), lambda i, pt, p=p: (pt[i, p], 0, 0))

    grid_spec = pltpu.PrefetchScalarGridSpec(
        num_scalar_prefetch=1,
        grid=(s,),
        in_specs=[pl.BlockSpec((1, 1, w), lambda i, pt: (i, 0, 0)),
                  pl.BlockSpec((n_pages * heads, 1), lambda i, pt: (0, 0))]
        + [page_spec(p) for p in range(n_pages)] * 2,
        out_specs=pl.BlockSpec((1, 1, w), lambda i, pt: (i, 0, 0)),
    )
    out = pl.pallas_call(
        functools.partial(_sb_decode_kernel, n_pages=n_pages, heads=heads, dh=dh),
        grid_spec=grid_spec,
        out_shape=jax.ShapeDtypeStruct((s, 1, w), F32),
        compiler_params=_params("parallel"),
        name="sb_decode",
    )(page_table, q3, bias_rows, *([pool_kt] * n_pages), *([pool_vt] * n_pages))
    return out.reshape(s, w)


def _out_proj_kernel(*refs, n_in):
    res_ref = refs[0]
    o_ref = refs[1 + 2 * n_in]
    acc = res_ref[...]
    for k in range(n_in):
        acc = acc + _dot(refs[1 + k][...].astype(BF16), refs[1 + n_in + k][...])
    o_ref[...] = acc


def _out_proj(res, xs, ws, tm):
    m, d = res.shape
    tm = _row_tile(m, tm)
    row = lambda i: (i, 0)
    fix = lambda i: (0, 0)
    return pl.pallas_call(
        functools.partial(_out_proj_kernel, n_in=len(xs)),
        grid=(m // tm,),
        in_specs=[pl.BlockSpec((tm, d), row)] + [pl.BlockSpec((tm, x.shape[1]), row) for x in xs]
        + [pl.BlockSpec(w.shape, fix) for w in ws],
        out_specs=pl.BlockSpec((tm, d), row),
        out_shape=jax.ShapeDtypeStruct((m, d), F32),
        compiler_params=_params("parallel"),
        name="out_proj",
    )(res, *xs, *ws)


def _swiglu_step(hb, wg, wu, wd):
    gate = _dot(hb, wg)
    act = gate * jax.nn.sigmoid(gate) * _dot(hb, wu)
    return _dot(act.astype(BF16), wd)


def _ffn_kernel(x_ref, g_ref, wg_ref, wu_ref, wd_ref, o_ref, h_ref, acc_ref):
    j = pl.program_id(1)

    @pl.when(j == 0)
    def _():
        h_ref[...] = _rms(x_ref[...], g_ref[...]).astype(BF16)
        acc_ref[...] = jnp.zeros_like(acc_ref)

    acc_ref[...] += _swiglu_step(h_ref[...], wg_ref[...], wu_ref[...], wd_ref[...])

    @pl.when(j == pl.num_programs(1) - 1)
    def _():
        o_ref[...] = x_ref[...] + acc_ref[...]


def _ffn(x, g, wg, wu, wd, tm, tf):
    m, d = x.shape
    f = wg.shape[1]
    tm = _row_tile(m, tm)
    tf = _row_tile(f, tf)
    row = lambda i, j: (i, 0)
    return pl.pallas_call(
        _ffn_kernel,
        grid=(m // tm, f // tf),
        in_specs=[pl.BlockSpec((tm, d), row), pl.BlockSpec((1, d), lambda i, j: (0, 0)),
                  pl.BlockSpec((d, tf), lambda i, j: (0, j)), pl.BlockSpec((d, tf), lambda i, j: (0, j)),
                  pl.BlockSpec((tf, d), lambda i, j: (j, 0))],
        out_specs=pl.BlockSpec((tm, d), row),
        out_shape=jax.ShapeDtypeStruct((m, d), F32),
        scratch_shapes=[pltpu.VMEM((tm, d), BF16), pltpu.VMEM((tm, d), F32)],
        compiler_params=_params("parallel", "arbitrary"),
        name="ffn",
    )(x, g, wg, wu, wd)


INFO_LANES = 128


def _router_kernel(x_ref, g_ref, whi_ref, wlo_ref, info_ref, cnt_ref, *, n_exp):
    @pl.when(pl.program_id(0) == 0)
    def _():
        cnt_ref[...] = jnp.zeros_like(cnt_ref)

    h = _rms(x_ref[...], g_ref[...])
    hi, lo = _split_bf16(h)
    logits = _dot(hi, whi_ref[...]) + (_dot(lo, whi_ref[...]) + _dot(hi, wlo_ref[...]))
    tm = logits.shape[0]
    lane = lax.broadcasted_iota(jnp.int32, logits.shape, 1).astype(F32)
    neg = jnp.float32(-jnp.inf)
    big = jnp.float32(logits.shape[1])
    l1 = jnp.where(lane < n_exp, logits, neg)
    m1 = jnp.max(l1, axis=1, keepdims=True)
    i1 = jnp.min(jnp.where(l1 == m1, lane, big), axis=1, keepdims=True)
    l2 = jnp.where(lane == i1, neg, l1)
    m2 = jnp.max(l2, axis=1, keepdims=True)
    i2 = jnp.min(jnp.where(l2 == m2, lane, big), axis=1, keepdims=True)
    e2 = jnp.exp(m2 - m1)
    den = 1.0 + e2
    sel = jnp.where(lane == i1, 1.0, 0.0) + jnp.where(lane == i2, 1.0, 0.0)
    r = lax.broadcasted_iota(jnp.int32, (tm, tm), 0)
    c = lax.broadcasted_iota(jnp.int32, (tm, tm), 1)
    earlier_rows = jnp.where(r > c, 1.0, 0.0).astype(BF16)
    before = _dot(earlier_rows, sel.astype(BF16)) + cnt_ref[...]
    r1 = jnp.sum(jnp.where(lane == i1, before, 0.0), axis=1, keepdims=True)
    r2 = jnp.sum(jnp.where(lane == i2, before, 0.0), axis=1, keepdims=True)
    fields = (i1, i2, 1.0 / den, e2 / den, r1, r2)
    info = jnp.zeros_like(logits)
    for k, f in enumerate(fields):
        info = jnp.where(lane == k, f, info)
    info_ref[...] = info
    cnt_ref[...] += jnp.sum(sel, axis=0, keepdims=True)


def _router(x, g, w_hi, w_lo, n_exp, tm):
    m, d = x.shape
    tm = _row_tile(m, tm)
    row = lambda i: (i, 0)
    fix = lambda i: (0, 0)
    return pl.pallas_call(
        functools.partial(_router_kernel, n_exp=n_exp),
        grid=(m // tm,),
        in_specs=[pl.BlockSpec((tm, d), row), pl.BlockSpec((1, d), fix),
                  pl.BlockSpec((d, INFO_LANES), fix), pl.BlockSpec((d, INFO_LANES), fix)],
        out_specs=[pl.BlockSpec((tm, INFO_LANES), row), pl.BlockSpec((1, INFO_LANES), fix)],
        out_shape=[jax.ShapeDtypeStruct((m, INFO_LANES), F32), jax.ShapeDtypeStruct((1, INFO_LANES), F32)],
        compiler_params=_params("arbitrary"),
        name="router",
    )(x, g, w_hi, w_lo)


def _routing_plan(info, cnt, n_exp, tg, n_tiles):
    i1, i2 = info[:, 0].astype(jnp.int32), info[:, 1].astype(jnp.int32)
    r1, r2 = info[:, 4].astype(jnp.int32), info[:, 5].astype(jnp.int32)
    counts = cnt[0, :n_exp].astype(jnp.int32)
    padded = (counts + tg - 1) // tg * tg
    ends = jnp.cumsum(padded)
    starts = ends - padded
    pos1 = jnp.take(starts, i1) + r1
    pos2 = jnp.take(starts, i2) + r2
    n_active = ends[n_exp - 1] // tg
    tiles = jnp.arange(n_tiles, dtype=jnp.int32)
    tile_expert = jnp.sum((tiles[:, None] * tg >= ends[None, :]).astype(jnp.int32), axis=1)
    tile_expert = jnp.minimum(tile_expert, jnp.take(tile_expert, jnp.maximum(n_active - 1, 0)))
    return pos1, pos2, tile_expert, n_active.reshape(1)


def _row_copies(copies_of, sem, n_rows):
    def start(r, carry):
        for src, dst in copies_of(r):
            pltpu.make_async_copy(src, dst, sem).start()
        return carry

    def wait(r, carry):
        for src, dst in copies_of(r):
            pltpu.make_async_copy(src, dst, sem).wait()
        return carry

    lax.fori_loop(0, n_rows, start, 0, unroll=8)
    lax.fori_loop(0, n_rows, wait, 0, unroll=8)


def _dispatch_kernel(pos1_ref, pos2_ref, x_ref, g_ref, init_ref, xs_ref, h_ref, sem):
    del init_ref
    tm = x_ref.shape[0]
    base = pl.program_id(0) * tm
    h_ref[...] = _rms(x_ref[...], g_ref[...])

    def copies_of(r):
        src = h_ref.at[pl.ds(r, 1), :]
        return [(src, xs_ref.at[pl.ds(pos_ref[base + r], 1), :]) for pos_ref in (pos1_ref, pos2_ref)]

    _row_copies(copies_of, sem, tm)


def _dispatch(x, g, pos1, pos2, n_slots, tm):
    m, d = x.shape
    tm = _row_tile(m, tm)
    grid_spec = pltpu.PrefetchScalarGridSpec(
        num_scalar_prefetch=2,
        grid=(m // tm,),
        in_specs=[pl.BlockSpec((tm, d), lambda i, p1, p2: (i, 0)), pl.BlockSpec((1, d), lambda i, p1, p2: (0, 0)),
                  pl.BlockSpec(memory_space=pl.ANY)],
        out_specs=pl.BlockSpec(memory_space=pl.ANY),
        scratch_shapes=[pltpu.VMEM((tm, d), F32), pltpu.SemaphoreType.DMA(())],
    )
    return pl.pallas_call(
        _dispatch_kernel,
        grid_spec=grid_spec,
        out_shape=jax.ShapeDtypeStruct((n_slots, d), F32),
        input_output_aliases={4: 0},
        compiler_params=_params("arbitrary"),
        name="moe_dispatch",
    )(pos1, pos2, x, g, jnp.zeros((n_slots, d), F32))


def _experts_kernel(te_ref, na_ref, x_ref, wg_ref, wu_ref, wd_ref, o_ref, h_ref, acc_ref):
    del te_ref
    i = pl.program_id(0)
    j = pl.program_id(1)
    last = pl.num_programs(1) - 1
    active = i < na_ref[0]

    @pl.when(active & (j == 0))
    def _():
        h_ref[...] = x_ref[...].astype(BF16)
        acc_ref[...] = jnp.zeros_like(acc_ref)

    @pl.when(active)
    def _():
        acc_ref[...] += _swiglu_step(h_ref[...], wg_ref[0], wu_ref[0], wd_ref[0])

    @pl.when(active & (j == last))
    def _():
        o_ref[...] = acc_ref[...]

    @pl.when(jnp.logical_not(active) & (j == last))
    def _():
        o_ref[...] = jnp.zeros_like(o_ref)


def _experts(xs, tile_expert, n_active, wg, wu, wd, tg, tf):
    n_slots, d = xs.shape
    f = wg.shape[2]
    tf = _row_tile(f, tf)
    nj = f // tf

    def col(i, j, te, na):
        return jnp.where(i < na[0], j, nj - 1)

    grid_spec = pltpu.PrefetchScalarGridSpec(
        num_scalar_prefetch=2,
        grid=(n_slots // tg, nj),
        in_specs=[pl.BlockSpec((tg, d), lambda i, j, te, na: (i, 0)),
                  pl.BlockSpec((1, d, tf), lambda i, j, te, na: (te[i], 0, col(i, j, te, na))),
                  pl.BlockSpec((1, d, tf), lambda i, j, te, na: (te[i], 0, col(i, j, te, na))),
                  pl.BlockSpec((1, tf, d), lambda i, j, te, na: (te[i], col(i, j, te, na), 0))],
        out_specs=pl.BlockSpec((tg, d), lambda i, j, te, na: (i, 0)),
        scratch_shapes=[pltpu.VMEM((tg, d), BF16), pltpu.VMEM((tg, d), F32)],
    )
    return pl.pallas_call(
        _experts_kernel,
        grid_spec=grid_spec,
        out_shape=jax.ShapeDtypeStruct((n_slots, d), F32),
        compiler_params=_params("parallel", "arbitrary"),
        name="moe_experts",
    )(tile_expert, n_active, xs, wg, wu, wd)


def _combine_kernel(pos1_ref, pos2_ref, x_ref, info_ref, ys_ref, o_ref, buf_ref, sem):
    tm = x_ref.shape[0]
    base = pl.program_id(0) * tm

    def copies_of(r):
        return [(ys_ref.at[pl.ds(pos_ref[base + r], 1), :], buf_ref.at[k, pl.ds(r, 1), :])
                for k, pos_ref in enumerate((pos1_ref, pos2_ref))]

    _row_copies(copies_of, sem, tm)
    o_ref[...] = x_ref[...] + info_ref[:, 2:3] * buf_ref[0] + info_ref[:, 3:4] * buf_ref[1]


def _combine(x, info, ys, pos1, pos2, tm):
    m, d = x.shape
    tm = _row_tile(m, tm)
    row = lambda i, p1, p2: (i, 0)
    grid_spec = pltpu.PrefetchScalarGridSpec(
        num_scalar_prefetch=2,
        grid=(m // tm,),
        in_specs=[pl.BlockSpec((tm, d), row), pl.BlockSpec((tm, INFO_LANES), row),
                  pl.BlockSpec(memory_space=pl.ANY)],
        out_specs=pl.BlockSpec((tm, d), row),
        scratch_shapes=[pltpu.VMEM((2, tm, d), F32), pltpu.SemaphoreType.DMA(())],
    )
    return pl.pallas_call(
        _combine_kernel,
        grid_spec=grid_spec,
        out_shape=jax.ShapeDtypeStruct((m, d), F32),
        compiler_params=_params("arbitrary"),
        name="moe_combine",
    )(pos1, pos2, x, info, ys)


def _moe(x, g, w_hi, w_lo, wg, wu, wd, n_exp, tm, tg, tf):
    m = x.shape[0]
    tg = min(tg, _row_tile(m, tg) * TOP_K)
    n_tiles = -(-(TOP_K * m) // tg) + n_exp
    info, cnt = _router(x, g, w_hi, w_lo, n_exp, tm)
    pos1, pos2, tile_expert, n_active = _routing_plan(info, cnt, n_exp, tg, n_tiles)
    xs = _dispatch(x, g, pos1, pos2, n_tiles * tg, tm)
    ys = _experts(xs, tile_expert, n_active, wg, wu, wd, tg, tf)
    return _combine(x, info, ys, pos1, pos2, tm)


def _norm_proj_kernel(x_ref, g_ref, w_ref, o_ref, h_ref):
    @pl.when(pl.program_id(1) == 0)
    def _():
        h_ref[...] = _rms(x_ref[...], g_ref[...]).astype(BF16)

    o_ref[...] = _dot(h_ref[...], w_ref[...])


def _norm_proj(x, g, w_bf, tm, tn):
    m, d = x.shape
    n = w_bf.shape[1]
    tm = _row_tile(m, tm)
    tn = _row_tile(n, tn)
    return pl.pallas_call(
        _norm_proj_kernel,
        grid=(m // tm, n // tn),
        in_specs=[pl.BlockSpec((tm, d), lambda i, j: (i, 0)), pl.BlockSpec((1, d), lambda i, j: (0, 0)),
                  pl.BlockSpec((d, tn), lambda i, j: (0, j))],
        out_specs=pl.BlockSpec((tm, tn), lambda i, j: (i, j)),
        out_shape=jax.ShapeDtypeStruct((m, n), F32),
        scratch_shapes=[pltpu.VMEM((tm, d), BF16)],
        compiler_params=_params("parallel", "arbitrary"),
        name="norm_proj",
    )(x, g, w_bf)


def _rotate(x, cos, sin_signed):
    return x * cos + pltpu.roll(x, x.shape[1] // 2, axis=1) * sin_signed


def _group_norm_gate(o, gate, gn_g, gn_b):
    mu = jnp.mean(o, axis=-1, keepdims=True)
    var = jnp.mean(jnp.square(o - mu), axis=-1, keepdims=True)
    on = (o - mu) * lax.rsqrt(var + EPS) * gn_g + gn_b
    return gate * jax.nn.sigmoid(gate) * on


def _ret_prompt_kernel(gc_ref, q_ref, k_ref, v_ref, gate_ref, cos_ref, sin_ref, dec_ref, qdec_ref, kdec_ref,
                       gng_ref, gnb_ref, y_ref, s_ref, *, heads, dk, dv):
    @pl.when(pl.program_id(1) == 0)
    def _():
        s_ref[...] = jnp.zeros_like(s_ref)

    cos = cos_ref[...]
    sin = sin_ref[...]
    hs = range(heads)
    kl = [slice(h * dk, (h + 1) * dk) for h in hs]
    vl = [slice(h * dv, (h + 1) * dv) for h in hs]
    qb = [_rotate(q_ref[:, kl[h]], cos, sin).astype(BF16) for h in hs]
    kr = [_rotate(k_ref[:, kl[h]], cos, sin) * (dk ** -0.5) for h in hs]
    vb = [v_ref[:, vl[h]].astype(BF16) for h in hs]
    state = [s_ref[0, h] for h in hs]
    scores = [_dot_nt(qb[h], kr[h].astype(BF16)) * dec_ref[h] for h in hs]
    cross = [_dot(qb[h], state[h].astype(BF16)) * qdec_ref[h] for h in hs]
    kv = [_dot((kr[h] * kdec_ref[h]).T.astype(BF16), vb[h]) for h in hs]
    for h in hs:
        s_ref[0, h] = gc_ref[h] * state[h] + kv[h]
    inner = [_dot(scores[h].astype(BF16), vb[h]) for h in hs]
    for h in hs:
        y_ref[:, vl[h]] = _group_norm_gate(inner[h] + cross[h], gate_ref[:, vl[h]], gng_ref[:, vl[h]],
                                           gnb_ref[:, vl[h]])


def _ret_prompt(proj, bsz, t, gc, cos, sin, dec, qdec, kdec, gn_g, gn_b, heads, dk, dv):
    m = proj.shape[0]
    c = dec.shape[1]
    nc = t // c
    qk, vw = heads * dk, heads * dv
    assert vw == 2 * qk
    tok = lambda col: (lambda b, ci: (b * nc + ci, col))
    fix2 = lambda b, ci: (0, 0)
    fix3 = lambda b, ci: (0, 0, 0)
    return pl.pallas_call(
        functools.partial(_ret_prompt_kernel, heads=heads, dk=dk, dv=dv),
        grid=(bsz, nc),
        in_specs=[pl.BlockSpec(memory_space=pltpu.SMEM),
                  pl.BlockSpec((c, qk), tok(0)), pl.BlockSpec((c, qk), tok(1)),
                  pl.BlockSpec((c, vw), tok(1)), pl.BlockSpec((c, vw), tok(2)),
                  pl.BlockSpec((c, dk), lambda b, ci: (ci, 0)), pl.BlockSpec((c, dk), lambda b, ci: (ci, 0)),
                  pl.BlockSpec(dec.shape, fix3), pl.BlockSpec(qdec.shape, fix3), pl.BlockSpec(kdec.shape, fix3),
                  pl.BlockSpec((1, vw), fix2), pl.BlockSpec((1, vw), fix2)],
        out_specs=[pl.BlockSpec((c, vw), lambda b, ci: (b * nc + ci, 0)),
                   pl.BlockSpec((1, heads, dk, dv), lambda b, ci: (b, 0, 0, 0))],
        out_shape=[jax.ShapeDtypeStruct((m, vw), F32), jax.ShapeDtypeStruct((bsz, heads, dk, dv), F32)],
        compiler_params=_params("parallel", "arbitrary"),
        name="ret_prompt",
    )(gc, proj, proj, proj, proj, cos, sin, dec, qdec, kdec, gn_g, gn_b)


def _ret_sample_kernel(gamma_ref, p_ref, s_ref, cos_ref, sin_ref, gng_ref, gnb_ref, y_ref, so_ref,
                       *, heads, dk, dv):
    qk, vw = heads * dk, heads * dv
    cos = cos_ref[...]
    sin = sin_ref[...]

    def column(x):
        col = jnp.broadcast_to(x, (dk, dk)).T
        return jnp.concatenate([col] * (dv // dk), axis=1)

    hs = range(heads)
    vl = [slice(h * dv, (h + 1) * dv) for h in hs]
    q = [_rotate(p_ref[0, :, h * dk:(h + 1) * dk], cos, sin) for h in hs]
    k = [_rotate(p_ref[0, :, qk + h * dk:qk + (h + 1) * dk], cos, sin) * (dk ** -0.5) for h in hs]
    v = [p_ref[0, :, 2 * qk + h * dv:2 * qk + (h + 1) * dv] for h in hs]
    q_col = [column(x) for x in q]
    k_col = [column(x) for x in k]
    qk_dot = [jnp.sum(q[h] * k[h], axis=-1, keepdims=True) for h in hs]
    for h in hs:
        state = s_ref[0, h]
        o = qk_dot[h] * v[h] + jnp.sum(q_col[h] * state, axis=0, keepdims=True) * gamma_ref[h]
        so_ref[0, h] = gamma_ref[h] * state + k_col[h] * v[h]
        gate = p_ref[0, :, 2 * qk + vw + h * dv:2 * qk + vw + (h + 1) * dv]
        y_ref[0, :, vl[h]] = _group_norm_gate(o, gate, gng_ref[:, vl[h]], gnb_ref[:, vl[h]])


def _ret_sample(proj, state, gamma, cos, sin, gn_g, gn_b, heads, dk, dv):
    s, n = proj.shape
    vw = heads * dv
    p3 = proj.reshape(s, 1, n)
    fix2 = lambda i: (0, 0)
    y, new_state = pl.pallas_call(
        functools.partial(_ret_sample_kernel, heads=heads, dk=dk, dv=dv),
        grid=(s,),
        in_specs=[pl.BlockSpec(memory_space=pltpu.SMEM),
                  pl.BlockSpec((1, 1, n), lambda i: (i, 0, 0)),
                  pl.BlockSpec((1, heads, dk, dv), lambda i: (i, 0, 0, 0)),
                  pl.BlockSpec((1, dk), fix2), pl.BlockSpec((1, dk), fix2),
                  pl.BlockSpec((1, vw), fix2), pl.BlockSpec((1, vw), fix2)],
        out_specs=[pl.BlockSpec((1, 1, vw), lambda i: (i, 0, 0)),
                   pl.BlockSpec((1, heads, dk, dv), lambda i: (i, 0, 0, 0))],
        out_shape=[jax.ShapeDtypeStruct((s, 1, vw), F32), jax.ShapeDtypeStruct(state.shape, F32)],
        compiler_params=_params("parallel"),
        name="ret_sample",
    )(gamma, p3, state, cos, sin, gn_g, gn_b)
    return y.reshape(s, vw), new_state


def _rope_tables(pos, dk):
    half = dk // 2
    inv_freq = ROPE_BASE ** (-jnp.linspace(0.0, 1.0, half, dtype=F32))
    ang = pos.astype(F32)[:, None] * inv_freq[None, :]
    cos, sin = jnp.cos(ang), jnp.sin(ang)
    return jnp.concatenate([cos, cos], axis=1), jnp.concatenate([-sin, sin], axis=1)


def _decay_tables(c, dk, dv):
    log_g = jnp.log1p(-jnp.exp2(-5.0 - jnp.arange(RET_HEADS, dtype=F32)))
    idx = jnp.arange(c, dtype=F32)
    diff = idx[:, None] - idx[None, :]
    dec = jnp.where(diff >= 0, jnp.exp(log_g[:, None, None] * jnp.maximum(diff, 0.0)), 0.0)
    qdec = jnp.exp(log_g[:, None] * (idx[None, :] + 1.0))
    kdec = jnp.exp(log_g[:, None] * (c - 1.0 - idx[None, :]))
    qdec = jnp.broadcast_to(qdec[:, :, None], (RET_HEADS, c, dv))
    kdec = jnp.broadcast_to(kdec[:, :, None], (RET_HEADS, c, dk))
    return dec, qdec, kdec, jnp.exp(log_g * c), jnp.exp(log_g)


def kernel(x_prompt, x_sample, cache_sb_k, cache_sb_v, cache_conv, state_ret, page_table, norm_mix_even, w_in_even, conv_w, conv_b, conv_ln_g, conv_ln_b, sb_q_norm, sb_k_norm, sb_bias, w_out_even, norm_ffn_even, ffn_w_gate, ffn_w_up, ffn_w_down, norm_mix_odd, w_in_odd, ret_gn_g, ret_gn_b, w_out_odd, norm_ffn_odd, router_w, moe_w_gate, moe_w_up, moe_w_down):
    bsz, t, d = x_prompt.shape
    s = x_sample.shape[0]
    assert x_sample.shape[1] == 1
    n_layers = w_in_even.shape[0] + w_in_odd.shape[0]
    heads, dh = sb_bias.shape[1], sb_q_norm.shape[1]
    w = heads * dh
    c = conv_w.shape[2]
    cw = conv_w.shape[1]
    page = cache_sb_k.shape[2]
    n_ctx = page_table.shape[1] * page
    dk = d // RET_HEADS
    dv = 2 * dk
    n_exp = router_w.shape[2]

    yp = x_prompt.reshape(bsz * t, d)
    ys = x_sample.reshape(s, d)
    row = lambda a: a.reshape(1, -1)

    head_mean = jnp.kron(jnp.eye(heads, dtype=F32), jnp.full((dh, dh), 1.0 / dh, F32)).astype(BF16)
    cos_p, sin_p = _rope_tables(jnp.arange(t, dtype=jnp.int32), dk)
    cos_s, sin_s = _rope_tables(jnp.full((1,), n_ctx, jnp.int32), dk)
    chunk = RET_CHUNK if t % RET_CHUNK == 0 else t
    dec, qdec, kdec, gamma_chunk, gamma = _decay_tables(chunk, dk, dv)

    kp_l, vp_l, ks_l, vs_l, cp_l, cs_l, rp_l, rs_l = [], [], [], [], [], [], [], []
    for layer in range(n_layers):
        li = layer // 2
        if layer % 2 == 0:
            w_in = w_in_even[li].astype(BF16)
            qg = row(jnp.tile(sb_q_norm[li], heads))
            kg = row(jnp.tile(sb_k_norm[li], heads))
            w_out = w_out_even[li].astype(BF16)
            conv_args = (conv_w[li], row(conv_b[li]), row(conv_ln_g[li]), row(conv_ln_b[li]))
            g_mix = row(norm_mix_even[li])

            up, kp, vp, qpb, kpb, vpb = _even_in(yp, g_mix, w_in, qg, kg, head_mean, c, w, dh, 512)
            us, k_s, v_s, qsb, _, _ = _even_in(ys, g_mix, w_in, qg, kg, head_mean, c, w, dh, 128)

            up3 = up.reshape(bsz, t, c)
            full = jnp.pad(up3, ((0, 0), (CONV_HEAD, 0), (0, 0)))
            cp = _conv_prompt(full, *conv_args, t, 256).reshape(bsz * t, c)
            cs = _conv_sample(cache_conv[li], us, *conv_args)

            op = _sb_prompt(sb_bias[li], qpb.reshape(bsz, t, w), kpb.reshape(bsz, t, w),
                            vpb.reshape(bsz, t, w), heads, dh, 256).reshape(bsz * t, w)
            n_pool = cache_sb_k.shape[1]
            n_pages = page_table.shape[1]
            pool_kt = cache_sb_k[li].transpose(0, 2, 3, 1).reshape(n_pool, w, page)
            pool_vt = cache_sb_v[li].transpose(0, 2, 3, 1).reshape(n_pool, w, page)
            os_ = _sb_decode(page_table, qsb, jnp.tile(sb_bias[li], n_pages).reshape(n_pages * heads, 1),
                             pool_kt, pool_vt, heads, dh)

            yp = _out_proj(yp, [cp, op], [w_out[:c], w_out[c:]], 1024)
            ys = _out_proj(ys, [cs, os_], [w_out[:c], w_out[c:]], 128)

            ffn_w = (ffn_w_gate[li].astype(BF16), ffn_w_up[li].astype(BF16), ffn_w_down[li].astype(BF16))
            g_ffn = row(norm_ffn_even[li])
            yp = _ffn(yp, g_ffn, *ffn_w, 1024, 256)
            ys = _ffn(ys, g_ffn, *ffn_w, 128, 256)

            kp_l.append(kp.reshape(bsz, t, heads, dh))
            vp_l.append(vp.reshape(bsz, t, heads, dh))
            ks_l.append(k_s.reshape(s, 1, heads, dh))
            vs_l.append(v_s.reshape(s, 1, heads, dh))
            cp_l.append(up3[:, t - (cw - 1):])
            cs_l.append(jnp.concatenate([cache_conv[li][:, 1:], us[:, None, :]], axis=1))
        else:
            w_in = w_in_odd[li].astype(BF16)
            w_out = w_out_odd[li].astype(BF16)
            g_mix = row(norm_mix_odd[li])
            gn_g, gn_b = row(ret_gn_g[li]), row(ret_gn_b[li])

            pp = _norm_proj(yp, g_mix, w_in, 1024, 2048)
            ps = _norm_proj(ys, g_mix, w_in, 128, 2048)
            mp, sp = _ret_prompt(pp, bsz, t, gamma_chunk, cos_p, sin_p, dec, qdec, kdec, gn_g, gn_b,
                                 RET_HEADS, dk, dv)
            ms, ss = _ret_sample(ps, state_ret[li], gamma, cos_s, sin_s, gn_g, gn_b, RET_HEADS, dk, dv)
            yp = _out_proj(yp, [mp], [w_out], 1024)
            ys = _out_proj(ys, [ms], [w_out], 128)

            g_ffn = row(norm_ffn_odd[li])
            rw = jnp.pad(router_w[li], ((0, 0), (0, INFO_LANES - n_exp)))
            rw_hi = rw.astype(BF16)
            rw_lo = (rw - rw_hi.astype(F32)).astype(BF16)
            moe_w = (moe_w_gate[li].astype(BF16), moe_w_up[li].astype(BF16), moe_w_down[li].astype(BF16))
            yp = _moe(yp, g_ffn, rw_hi, rw_lo, *moe_w, n_exp, 512, 512, 512)
            ys = _moe(ys, g_ffn, rw_hi, rw_lo, *moe_w, n_exp, 128, 512, 512)
            rp_l.append(sp)
            rs_l.append(ss)

    return (yp.reshape(bsz, t, d), ys.reshape(s, 1, d), jnp.stack(kp_l), jnp.stack(vp_l), jnp.stack(ks_l),
            jnp.stack(vs_l), jnp.stack(cp_l), jnp.stack(cs_l), jnp.stack(rp_l), jnp.stack(rs_l))
```

```python
import functools

import jax
import jax.numpy as jnp
from jax import lax
from jax.experimental import pallas as pl
from jax.experimental.pallas import tpu as pltpu

F32 = jnp.float32
BF16 = jnp.bfloat16

EPS = 1e-6
RET_HEADS = 8
RET_CHUNK = 128
ROPE_BASE = 10000.0
TOP_K = 2

V7X_VMEM_LIMIT_BYTES = 56 * 1024 * 1024


def _params(*sem):
    return pltpu.CompilerParams(dimension_semantics=sem, vmem_limit_bytes=V7X_VMEM_LIMIT_BYTES)


def _row_tile(m, want):
    t = min(m, want)
    while m % t:
        t //= 2
    return t


def _rms(x, g):
    return x * lax.rsqrt(jnp.mean(x * x, axis=-1, keepdims=True) + EPS) * g


def _split_bf16(x):
    hi = x.astype(BF16)
    lo = (x - hi.astype(F32)).astype(BF16)
    return hi, lo


LOG2_E = 1.4426950408889634


def _log2_sigmoid(z2):
    return jnp.minimum(z2, 0.0) - jnp.log2(1.0 + jnp.exp2(-jnp.abs(z2)))


def _dot(a, b):
    return jnp.dot(a, b, preferred_element_type=F32)


def _dot_nt(a, b):
    return lax.dot_general(a, b, (((1,), (1,)), ((), ())), preferred_element_type=F32)


def _even_in_kernel(x_ref, g_ref, w_ref, wkv_t_ref, qg_ref, kg_ref, hm_ref,
                    u_ref, qb_ref, kt_ref, vt_ref, ktb_ref, vtb_ref, *, c, w, dh, qscale):
    hb = _rms(x_ref[...], g_ref[...]).astype(BF16)
    a = _dot(hb, w_ref[:, 0:c])
    gate = _dot(hb, w_ref[:, c:2 * c])
    u_ref[...] = a * jax.nn.sigmoid(gate)
    q = _dot(hb, w_ref[:, 2 * c:2 * c + w])
    hi, lo = _split_bf16(q * q)
    ms = _dot(hi, hm_ref[...]) + _dot(lo, hm_ref[...])
    qb_ref[...] = (q * lax.rsqrt(ms + EPS) * (qg_ref[...] * qscale)).astype(BF16)
    kt = _dot_nt(wkv_t_ref[0:w, :], hb)
    for h in range(w // dh):
        rows = slice(h * dh, (h + 1) * dh)
        kh = kt[rows]
        kh = kh * lax.rsqrt(jnp.mean(kh * kh, axis=0, keepdims=True) + EPS) * kg_ref[...]
        kt_ref[0, rows, :] = kh
        ktb_ref[0, rows, :] = kh.astype(BF16)
    vt = _dot_nt(wkv_t_ref[w:2 * w, :], hb)
    vt_ref[0] = vt
    vtb_ref[0] = vt.astype(BF16)


def _even_in(x, n_seq, g, w_bf, wkv_t, qg_t, kg_col, hm, c, w, dh, tm):
    m, d = x.shape
    t = m // n_seq
    tm = _row_tile(t, tm)
    nt = t // tm
    row = lambda i: (i, 0)
    fix = lambda i: (0, 0)
    col = lambda i: (i // nt, 0, i % nt)
    out_t = lambda dt: jax.ShapeDtypeStruct((n_seq, w, t), dt)
    return pl.pallas_call(
        functools.partial(_even_in_kernel, c=c, w=w, dh=dh, qscale=dh ** -0.5 * LOG2_E),
        grid=(m // tm,),
        in_specs=[pl.BlockSpec((tm, d), row), pl.BlockSpec((1, d), fix), pl.BlockSpec(w_bf.shape, fix),
                  pl.BlockSpec(wkv_t.shape, fix), pl.BlockSpec((1, w), fix), pl.BlockSpec((dh, 1), fix),
                  pl.BlockSpec((w, w), fix)],
        out_specs=[pl.BlockSpec((tm, c), row), pl.BlockSpec((tm, w), row)] + [pl.BlockSpec((1, w, tm), col)] * 4,
        out_shape=[jax.ShapeDtypeStruct((m, c), F32), jax.ShapeDtypeStruct((m, w), BF16),
                   out_t(F32), out_t(F32), out_t(BF16), out_t(BF16)],
        compiler_params=_params("parallel"),
        name="even_in",
    )(x, g, w_bf, wkv_t, qg_t, kg_col, hm)


CONV_HEAD = 32


def _ln_silu(cv, g, b):
    mu = jnp.mean(cv, axis=-1, keepdims=True)
    var = jnp.mean(jnp.square(cv - mu), axis=-1, keepdims=True)
    y = (cv - mu) * lax.rsqrt(var + EPS) * g + b
    return y * jax.nn.sigmoid(y)


def _conv_prompt_kernel(full_ref, w_ref, b_ref, g_ref, beta_ref, o_ref, *, tt, cw):
    t = pl.program_id(1)
    n = tt + CONV_HEAD
    win = full_ref[0, pl.ds(pl.multiple_of(t * tt, tt), n), :]
    acc = jnp.zeros((tt, win.shape[1]), F32)
    for j in range(cw):
        off = j + CONV_HEAD - (cw - 1)
        shifted = win if off == 0 else pltpu.roll(win, n - off, axis=0)
        acc = acc + shifted[:tt] * w_ref[j:j + 1, :]
    o_ref[0] = _ln_silu(acc + b_ref[...], g_ref[...], beta_ref[...])


def _conv_prompt(full, w, b, g, beta, t_len, tt):
    bsz, tf, c = full.shape
    cw = w.shape[0]
    tt = _row_tile(t_len, tt)
    fix = lambda i, t: (0, 0)
    return pl.pallas_call(
        functools.partial(_conv_prompt_kernel, tt=tt, cw=cw),
        grid=(bsz, t_len // tt),
        in_specs=[pl.BlockSpec((1, tf, c), lambda i, t: (i, 0, 0)), pl.BlockSpec((cw, c), fix),
                  pl.BlockSpec((1, c), fix), pl.BlockSpec((1, c), fix), pl.BlockSpec((1, c), fix)],
        out_specs=pl.BlockSpec((1, tt, c), lambda i, t: (i, t, 0)),
        out_shape=jax.ShapeDtypeStruct((bsz, t_len, c), F32),
        compiler_params=_params("parallel", "arbitrary"),
        name="conv_prompt",
    )(full, w, b, g, beta)


def _conv_sample_kernel(buf_ref, u_ref, w_ref, b_ref, g_ref, beta_ref, o_ref, *, cw):
    past = jnp.sum(buf_ref[...] * w_ref[0:cw - 1, :][None], axis=1)
    cv = past + u_ref[...] * w_ref[cw - 1:cw, :] + b_ref[...]
    o_ref[...] = _ln_silu(cv, g_ref[...], beta_ref[...])


def _conv_sample(buf, u, w, b, g, beta):
    s, nb, c = buf.shape
    cw = w.shape[0]
    ts = _row_tile(s, 8)
    fix = lambda i: (0, 0)
    return pl.pallas_call(
        functools.partial(_conv_sample_kernel, cw=cw),
        grid=(s // ts,),
        in_specs=[pl.BlockSpec((ts, nb, c), lambda i: (i, 0, 0)), pl.BlockSpec((ts, c), lambda i: (i, 0)),
                  pl.BlockSpec((cw, c), fix), pl.BlockSpec((1, c), fix), pl.BlockSpec((1, c), fix),
                  pl.BlockSpec((1, c), fix)],
        out_specs=pl.BlockSpec((ts, c), lambda i: (i, 0)),
        out_shape=jax.ShapeDtypeStruct((s, c), F32),
        compiler_params=_params("parallel"),
        name="conv_sample",
    )(buf, u, w, b, g, beta)


def _strict_lower_ones(n, repeat=1):
    r = lax.broadcasted_iota(jnp.int32, (repeat * n, n), 0) % n
    c = lax.broadcasted_iota(jnp.int32, (repeat * n, n), 1)
    return jnp.where(r > c, 1.0, 0.0).astype(BF16)


def _suffix_after(lk, ones_after2):
    return _dot(jnp.concatenate(_split_bf16(lk), axis=1), ones_after2)


def _sb_blocks(qs, k_blks, v_blks, biases2, runs, ones_after2, mask):
    n = len(qs)
    z2 = [_dot(qs[g], k_blks[g]) + biases2[g] for g in range(n)]
    ls = [_log2_sigmoid(z) for z in z2]
    lk = [l - z for l, z in zip(ls, z2)]
    if mask is not None:
        lk = [jnp.where(mask, x, 0.0) for x in lk]
    suf = [_suffix_after(x, ones_after2) for x in lk]
    a = [jnp.exp2(ls[g] + suf[g] + runs[g]) for g in range(n)]
    if mask is not None:
        a = [jnp.where(mask, x, 0.0) for x in a]
    pv = [_dot_nt(a[g].astype(BF16), v_blks[g]) for g in range(n)]
    return pv, [runs[g] + (suf[g][:, 0:1] + lk[g][:, 0:1]) for g in range(n)]


def _sb_prompt_kernel(bias_ref, q_ref, k_ref, v_ref, o_ref, *, tq, heads, dh, group):
    i = pl.program_id(1)
    row = lax.broadcasted_iota(jnp.int32, (tq, tq), 0)
    col = lax.broadcasted_iota(jnp.int32, (tq, tq), 1)
    causal = row > col
    ones_after = _strict_lower_ones(tq, repeat=2)
    for h0 in range(0, heads, group):
        lanes = [slice(h * dh, (h + 1) * dh) for h in range(h0, h0 + group)]
        qs = [q_ref[0, :, l] for l in lanes]
        biases = [bias_ref[h] * LOG2_E for h in range(h0, h0 + group)]

        def tile(j, carry, mask, lanes=lanes, qs=qs, biases=biases):
            keys = pl.ds(pl.multiple_of(j * tq, tq), tq)
            pv, runs = _sb_blocks(qs, [k_ref[0, l, keys] for l in lanes], [v_ref[0, l, keys] for l in lanes],
                                  biases, [c[1] for c in carry], ones_after, mask)
            return tuple((carry[g][0] + pv[g], runs[g]) for g in range(group))

        init = tuple((jnp.zeros((tq, dh), F32), jnp.zeros((tq, 1), F32)) for _ in range(group))
        carry = tile(i, init, causal)
        carry = lax.fori_loop(1, i + 1, lambda jj, c, tile=tile: tile(i - jj, c, None), carry)
        for g in range(group):
            o_ref[0, :, lanes[g]] = carry[g][0]


def _sb_prompt(bias, qb, ktb, vtb, heads, dh, tq):
    bsz, t, w = qb.shape
    tq = _row_tile(t, tq)
    seq = lambda b, i: (b, 0, 0)
    return pl.pallas_call(
        functools.partial(_sb_prompt_kernel, tq=tq, heads=heads, dh=dh, group=4),
        grid=(bsz, t // tq),
        in_specs=[pl.BlockSpec(memory_space=pltpu.SMEM),
                  pl.BlockSpec((1, tq, w), lambda b, i: (b, i, 0)),
                  pl.BlockSpec((1, w, t), seq), pl.BlockSpec((1, w, t), seq)],
        out_specs=pl.BlockSpec((1, tq, w), lambda b, i: (b, i, 0)),
        out_shape=jax.ShapeDtypeStruct((bsz, t, w), F32),
        compiler_params=_params("parallel", "arbitrary"),
        name="sb_prompt",
    )(bias, qb, ktb, vtb)


def _sb_decode_kernel(pt_ref, q_ref, bias_ref, *refs, n_pages, heads, dh):
    del pt_ref
    k_refs = refs[:n_pages]
    v_refs = refs[n_pages:2 * n_pages]
    o_ref = refs[2 * n_pages]
    w = heads * dh
    page = k_refs[0].shape[2]
    rows = n_pages * heads
    head_of_lane = lax.broadcasted_iota(jnp.int32, (heads, w), 1) // dh
    head_mask = head_of_lane == lax.broadcasted_iota(jnp.int32, (heads, w), 0)
    q_bd = jnp.where(head_mask, jnp.broadcast_to(q_ref[0].astype(F32), (heads, w)), 0.0).astype(BF16)
    z = jnp.concatenate([_dot(q_bd, k_refs[p][0].astype(BF16)) for p in range(n_pages)], axis=0)
    z2 = z + bias_ref[...] * LOG2_E
    ls = _log2_sigmoid(z2)
    lk = ls - z2
    hi, lo = _split_bf16(lk)
    ones_after = _strict_lower_ones(page)
    within = _dot(hi, ones_after) + _dot(lo, ones_after)
    all_ones = jnp.ones((page, page), BF16)
    t_hi, t_lo = _split_bf16(_dot(hi, all_ones) + _dot(lo, all_ones))
    r = lax.broadcasted_iota(jnp.int32, (rows, rows), 0)
    c = lax.broadcasted_iota(jnp.int32, (rows, rows), 1)
    later_page = jnp.where((r % heads == c % heads) & (c // heads > r // heads), 1.0, 0.0).astype(BF16)
    run = _dot(later_page, t_hi) + _dot(later_page, t_lo)
    a = jnp.exp2(ls + within + run).astype(BF16)
    acc = jnp.zeros((heads, w), F32)
    for p in range(n_pages):
        acc = acc + _dot_nt(a[p * heads:(p + 1) * heads], v_refs[p][0].astype(BF16))
    o_ref[0] = jnp.sum(jnp.where(head_mask, acc, 0.0), axis=0, keepdims=True)


def _sb_decode(page_table, qb, bias_rows, pool_kt, pool_vt, heads, dh):
    s, n_pages = page_table.shape
    _, w, page = pool_kt.shape
    q3 = qb.reshape(s, 1, w)

    def page_spec(p):
        return pl.BlockSpec((1, w, page), lambda i, pt, p=p: (pt[i, p], 0, 0))

    grid_spec = pltpu.PrefetchScalarGridSpec(
        num_scalar_prefetch=1,
        grid=(s,),
        in_specs=[pl.BlockSpec((1, 1, w), lambda i, pt: (i, 0, 0)),
                  pl.BlockSpec((n_pages * heads, 1), lambda i, pt: (0, 0))]
        + [page_spec(p) for p in range(n_pages)] * 2,
        out_specs=pl.BlockSpec((1, 1, w), lambda i, pt: (i, 0, 0)),
    )
    out = pl.pallas_call(
        functools.partial(_sb_decode_kernel, n_pages=n_pages, heads=heads, dh=dh),
        grid_spec=grid_spec,
        out_shape=jax.ShapeDtypeStruct((s, 1, w), F32),
        compiler_params=_params("parallel"),
        name="sb_decode",
    )(page_table, q3, bias_rows, *([pool_kt] * n_pages), *([pool_vt] * n_pages))
    return out.reshape(s, w)


def _out_proj_kernel(*refs, n_in):
    res_ref = refs[0]
    o_ref = refs[1 + 2 * n_in]
    acc = res_ref[...]
    for k in range(n_in):
        acc = acc + _dot(refs[1 + k][...].astype(BF16), refs[1 + n_in + k][...])
    o_ref[...] = acc


def _out_proj(res, xs, ws, tm):
    m, d = res.shape
    tm = _row_tile(m, tm)
    row = lambda i: (i, 0)
    fix = lambda i: (0, 0)
    return pl.pallas_call(
        functools.partial(_out_proj_kernel, n_in=len(xs)),
        grid=(m // tm,),
        in_specs=[pl.BlockSpec((tm, d), row)] + [pl.BlockSpec((tm, x.shape[1]), row) for x in xs]
        + [pl.BlockSpec(w.shape, fix) for w in ws],
        out_specs=pl.BlockSpec((tm, d), row),
        out_shape=jax.ShapeDtypeStruct((m, d), F32),
        compiler_params=_params("parallel"),
        name="out_proj",
    )(res, *xs, *ws)


def _swiglu_step(hb, wg, wu, wd):
    gate = _dot(hb, wg)
    act = gate * jax.nn.sigmoid(gate) * _dot(hb, wu)
    return _dot(act.astype(BF16), wd)


def _ffn_kernel(x_ref, g_ref, wg_ref, wu_ref, wd_ref, o_ref, h_ref, acc_ref):
    j = pl.program_id(1)

    @pl.when(j == 0)
    def _():
        h_ref[...] = _rms(x_ref[...], g_ref[...]).astype(BF16)
        acc_ref[...] = jnp.zeros_like(acc_ref)

    acc_ref[...] += _swiglu_step(h_ref[...], wg_ref[...], wu_ref[...], wd_ref[...])

    @pl.when(j == pl.num_programs(1) - 1)
    def _():
        o_ref[...] = x_ref[...] + acc_ref[...]


def _ffn(x, g, wg, wu, wd, tm, tf):
    m, d = x.shape
    f = wg.shape[1]
    tm = _row_tile(m, tm)
    tf = _row_tile(f, tf)
    row = lambda i, j: (i, 0)
    return pl.pallas_call(
        _ffn_kernel,
        grid=(m // tm, f // tf),
        in_specs=[pl.BlockSpec((tm, d), row), pl.BlockSpec((1, d), lambda i, j: (0, 0)),
                  pl.BlockSpec((d, tf), lambda i, j: (0, j)), pl.BlockSpec((d, tf), lambda i, j: (0, j)),
                  pl.BlockSpec((tf, d), lambda i, j: (j, 0))],
        out_specs=pl.BlockSpec((tm, d), row),
        out_shape=jax.ShapeDtypeStruct((m, d), F32),
        scratch_shapes=[pltpu.VMEM((tm, d), BF16), pltpu.VMEM((tm, d), F32)],
        compiler_params=_params("parallel", "arbitrary"),
        name="ffn",
    )(x, g, wg, wu, wd)


INFO_LANES = 128


def _router_kernel(x_ref, g_ref, whi_ref, wlo_ref, info_ref, cnt_ref, *, n_exp):
    @pl.when(pl.program_id(0) == 0)
    def _():
        cnt_ref[...] = jnp.zeros_like(cnt_ref)

    h = _rms(x_ref[...], g_ref[...])
    hi, lo = _split_bf16(h)
    logits = _dot(hi, whi_ref[...]) + (_dot(lo, whi_ref[...]) + _dot(hi, wlo_ref[...]))
    tm = logits.shape[0]
    lane = lax.broadcasted_iota(jnp.int32, logits.shape, 1).astype(F32)
    neg = jnp.float32(-jnp.inf)
    big = jnp.float32(logits.shape[1])
    l1 = jnp.where(lane < n_exp, logits, neg)
    m1 = jnp.max(l1, axis=1, keepdims=True)
    i1 = jnp.min(jnp.where(l1 == m1, lane, big), axis=1, keepdims=True)
    l2 = jnp.where(lane == i1, neg, l1)
    m2 = jnp.max(l2, axis=1, keepdims=True)
    i2 = jnp.min(jnp.where(l2 == m2, lane, big), axis=1, keepdims=True)
    e2 = jnp.exp(m2 - m1)
    den = 1.0 + e2
    sel = jnp.where(lane == i1, 1.0, 0.0) + jnp.where(lane == i2, 1.0, 0.0)
    r = lax.broadcasted_iota(jnp.int32, (tm, tm), 0)
    c = lax.broadcasted_iota(jnp.int32, (tm, tm), 1)
    earlier_rows = jnp.where(r > c, 1.0, 0.0).astype(BF16)
    before = _dot(earlier_rows, sel.astype(BF16)) + cnt_ref[...]
    r1 = jnp.sum(jnp.where(lane == i1, before, 0.0), axis=1, keepdims=True)
    r2 = jnp.sum(jnp.where(lane == i2, before, 0.0), axis=1, keepdims=True)
    fields = (i1, i2, 1.0 / den, e2 / den, r1, r2)
    info = jnp.zeros_like(logits)
    for k, f in enumerate(fields):
        info = jnp.where(lane == k, f, info)
    info_ref[...] = info
    cnt_ref[...] += jnp.sum(sel, axis=0, keepdims=True)


def _router(x, g, w_hi, w_lo, n_exp, tm):
    m, d = x.shape
    tm = _row_tile(m, tm)
    row = lambda i: (i, 0)
    fix = lambda i: (0, 0)
    return pl.pallas_call(
        functools.partial(_router_kernel, n_exp=n_exp),
        grid=(m // tm,),
        in_specs=[pl.BlockSpec((tm, d), row), pl.BlockSpec((1, d), fix),
                  pl.BlockSpec((d, INFO_LANES), fix), pl.BlockSpec((d, INFO_LANES), fix)],
        out_specs=[pl.BlockSpec((tm, INFO_LANES), row), pl.BlockSpec((1, INFO_LANES), fix)],
        out_shape=[jax.ShapeDtypeStruct((m, INFO_LANES), F32), jax.ShapeDtypeStruct((1, INFO_LANES), F32)],
        compiler_params=_params("arbitrary"),
        name="router",
    )(x, g, w_hi, w_lo)


def _routing_plan(info, cnt, n_exp, tg, n_tiles):
    i1, i2 = info[:, 0].astype(jnp.int32), info[:, 1].astype(jnp.int32)
    r1, r2 = info[:, 4].astype(jnp.int32), info[:, 5].astype(jnp.int32)
    counts = cnt[0, :n_exp].astype(jnp.int32)
    padded = (counts + tg - 1) // tg * tg
    ends = jnp.cumsum(padded)
    starts = ends - padded
    pos1 = jnp.take(starts, i1) + r1
    pos2 = jnp.take(starts, i2) + r2
    n_active = ends[n_exp - 1] // tg
    tiles = jnp.arange(n_tiles, dtype=jnp.int32)
    tile_expert = jnp.sum((tiles[:, None] * tg >= ends[None, :]).astype(jnp.int32), axis=1)
    tile_expert = jnp.minimum(tile_expert, jnp.take(tile_expert, jnp.maximum(n_active - 1, 0)))
    return pos1, pos2, tile_expert, n_active.reshape(1)


def _row_copies(copies_of, sem, n_rows):
    def start(r, carry):
        for src, dst in copies_of(r):
            pltpu.make_async_copy(src, dst, sem).start()
        return carry

    def wait(r, carry):
        for src, dst in copies_of(r):
            pltpu.make_async_copy(src, dst, sem).wait()
        return carry

    lax.fori_loop(0, n_rows, start, 0, unroll=8)
    lax.fori_loop(0, n_rows, wait, 0, unroll=8)


def _dispatch_kernel(pos1_ref, pos2_ref, x_ref, g_ref, init_ref, xs_ref, h_ref, sem):
    del init_ref
    tm = x_ref.shape[0]
    base = pl.program_id(0) * tm
    h_ref[...] = _rms(x_ref[...], g_ref[...])

    def copies_of(r):
        src = h_ref.at[pl.ds(r, 1), :]
        return [(src, xs_ref.at[pl.ds(pos_ref[base + r], 1), :]) for pos_ref in (pos1_ref, pos2_ref)]

    _row_copies(copies_of, sem, tm)


def _dispatch(x, g, pos1, pos2, n_slots, tm):
    m, d = x.shape
    tm = _row_tile(m, tm)
    grid_spec = pltpu.PrefetchScalarGridSpec(
        num_scalar_prefetch=2,
        grid=(m // tm,),
        in_specs=[pl.BlockSpec((tm, d), lambda i, p1, p2: (i, 0)), pl.BlockSpec((1, d), lambda i, p1, p2: (0, 0)),
                  pl.BlockSpec(memory_space=pl.ANY)],
        out_specs=pl.BlockSpec(memory_space=pl.ANY),
        scratch_shapes=[pltpu.VMEM((tm, d), F32), pltpu.SemaphoreType.DMA(())],
    )
    return pl.pallas_call(
        _dispatch_kernel,
        grid_spec=grid_spec,
        out_shape=jax.ShapeDtypeStruct((n_slots, d), F32),
        input_output_aliases={4: 0},
        compiler_params=_params("arbitrary"),
        name="moe_dispatch",
    )(pos1, pos2, x, g, jnp.zeros((n_slots, d), F32))


def _experts_kernel(te_ref, na_ref, x_ref, wg_ref, wu_ref, wd_ref, o_ref, h_ref, acc_ref):
    del te_ref
    i = pl.program_id(0)
    j = pl.program_id(1)
    last = pl.num_programs(1) - 1
    active = i < na_ref[0]

    @pl.when(active & (j == 0))
    def _():
        h_ref[...] = x_ref[...].astype(BF16)
        acc_ref[...] = jnp.zeros_like(acc_ref)

    @pl.when(active)
    def _():
        acc_ref[...] += _swiglu_step(h_ref[...], wg_ref[0], wu_ref[0], wd_ref[0])

    @pl.when(active & (j == last))
    def _():
        o_ref[...] = acc_ref[...]

    @pl.when(jnp.logical_not(active) & (j == last))
    def _():
        o_ref[...] = jnp.zeros_like(o_ref)


def _experts(xs, tile_expert, n_active, wg, wu, wd, tg, tf):
    n_slots, d = xs.shape
    f = wg.shape[2]
    tf = _row_tile(f, tf)
    nj = f // tf

    def col(i, j, te, na):
        return jnp.where(i < na[0], j, nj - 1)

    grid_spec = pltpu.PrefetchScalarGridSpec(
        num_scalar_prefetch=2,
        grid=(n_slots // tg, nj),
        in_specs=[pl.BlockSpec((tg, d), lambda i, j, te, na: (i, 0)),
                  pl.BlockSpec((1, d, tf), lambda i, j, te, na: (te[i], 0, col(i, j, te, na))),
                  pl.BlockSpec((1, d, tf), lambda i, j, te, na: (te[i], 0, col(i, j, te, na))),
                  pl.BlockSpec((1, tf, d), lambda i, j, te, na: (te[i], col(i, j, te, na), 0))],
        out_specs=pl.BlockSpec((tg, d), lambda i, j, te, na: (i, 0)),
        scratch_shapes=[pltpu.VMEM((tg, d), BF16), pltpu.VMEM((tg, d), F32)],
    )
    return pl.pallas_call(
        _experts_kernel,
        grid_spec=grid_spec,
        out_shape=jax.ShapeDtypeStruct((n_slots, d), F32),
        compiler_params=_params("parallel", "arbitrary"),
        name="moe_experts",
    )(tile_expert, n_active, xs, wg, wu, wd)


def _combine_kernel(pos1_ref, pos2_ref, x_ref, info_ref, ys_ref, o_ref, buf_ref, sem):
    tm = x_ref.shape[0]
    base = pl.program_id(0) * tm

    def copies_of(r):
        return [(ys_ref.at[pl.ds(pos_ref[base + r], 1), :], buf_ref.at[k, pl.ds(r, 1), :])
                for k, pos_ref in enumerate((pos1_ref, pos2_ref))]

    _row_copies(copies_of, sem, tm)
    o_ref[...] = x_ref[...] + info_ref[:, 2:3] * buf_ref[0] + info_ref[:, 3:4] * buf_ref[1]


def _combine(x, info, ys, pos1, pos2, tm):
    m, d = x.shape
    tm = _row_tile(m, tm)
    row = lambda i, p1, p2: (i, 0)
    grid_spec = pltpu.PrefetchScalarGridSpec(
        num_scalar_prefetch=2,
        grid=(m // tm,),
        in_specs=[pl.BlockSpec((tm, d), row), pl.BlockSpec((tm, INFO_LANES), row),
                  pl.BlockSpec(memory_space=pl.ANY)],
        out_specs=pl.BlockSpec((tm, d), row),
        scratch_shapes=[pltpu.VMEM((2, tm, d), F32), pltpu.SemaphoreType.DMA(())],
    )
    return pl.pallas_call(
        _combine_kernel,
        grid_spec=grid_spec,
        out_shape=jax.ShapeDtypeStruct((m, d), F32),
        compiler_params=_params("arbitrary"),
        name="moe_combine",
    )(pos1, pos2, x, info, ys)


def _moe(x, g, w_hi, w_lo, wg, wu, wd, n_exp, tm, tg, tf):
    m = x.shape[0]
    tg = min(tg, _row_tile(m, tg) * TOP_K)
    n_tiles = -(-(TOP_K * m) // tg) + n_exp
    info, cnt = _router(x, g, w_hi, w_lo, n_exp, tm)
    pos1, pos2, tile_expert, n_active = _routing_plan(info, cnt, n_exp, tg, n_tiles)
    xs = _dispatch(x, g, pos1, pos2, n_tiles * tg, tm)
    ys = _experts(xs, tile_expert, n_active, wg, wu, wd, tg, tf)
    return _combine(x, info, ys, pos1, pos2, tm)


def _norm_proj_kernel(x_ref, g_ref, w_ref, of_ref, ob_ref, h_ref, *, nf):
    j = pl.program_id(1)

    @pl.when(j == 0)
    def _():
        h_ref[...] = _rms(x_ref[...], g_ref[...]).astype(BF16)

    @pl.when(j < nf)
    def _():
        of_ref[...] = _dot(h_ref[...], w_ref[...])

    @pl.when(j >= nf)
    def _():
        ob_ref[...] = _dot(h_ref[...], w_ref[...]).astype(BF16)


def _norm_proj(x, g, w_bf, n_f32, tm, tn):
    m, d = x.shape
    n = w_bf.shape[1]
    tm = _row_tile(m, tm)
    assert n_f32 % tn == 0 and (n - n_f32) % tn == 0 and 0 < n_f32 < n
    nf = n_f32 // tn
    return pl.pallas_call(
        functools.partial(_norm_proj_kernel, nf=nf),
        grid=(m // tm, n // tn),
        in_specs=[pl.BlockSpec((tm, d), lambda i, j: (i, 0)), pl.BlockSpec((1, d), lambda i, j: (0, 0)),
                  pl.BlockSpec((d, tn), lambda i, j: (0, j))],
        out_specs=[pl.BlockSpec((tm, tn), lambda i, j: (i, jnp.minimum(j, nf - 1))),
                   pl.BlockSpec((tm, tn), lambda i, j: (i, jnp.maximum(j - nf, 0)))],
        out_shape=[jax.ShapeDtypeStruct((m, n_f32), F32), jax.ShapeDtypeStruct((m, n - n_f32), BF16)],
        scratch_shapes=[pltpu.VMEM((tm, d), BF16)],
        compiler_params=_params("parallel", "arbitrary"),
        name="norm_proj",
    )(x, g, w_bf)


def _rotate(x, cos, sin_signed):
    return x * cos + pltpu.roll(x, x.shape[1] // 2, axis=1) * sin_signed


def _group_norm_gate(o, gate, gn_g, gn_b):
    mu = jnp.mean(o, axis=-1, keepdims=True)
    var = jnp.mean(jnp.square(o - mu), axis=-1, keepdims=True)
    on = (o - mu) * lax.rsqrt(var + EPS) * gn_g + gn_b
    return gate * jax.nn.sigmoid(gate) * on


def _ret_prompt_kernel(gc_ref, q_ref, k_ref, v_ref, gate_ref, cos_ref, sin_ref, dec_ref, qdec_ref, kdec_ref,
                       gng_ref, gnb_ref, y_ref, s_ref, *, heads, dk, dv):
    @pl.when(pl.program_id(1) == 0)
    def _():
        s_ref[...] = jnp.zeros_like(s_ref)

    cos = cos_ref[...]
    sin = sin_ref[...]
    hs = range(heads)
    kl = [slice(h * dk, (h + 1) * dk) for h in hs]
    vl = [slice(h * dv, (h + 1) * dv) for h in hs]
    qb = [_rotate(q_ref[:, kl[h]], cos, sin).astype(BF16) for h in hs]
    kr = [_rotate(k_ref[:, kl[h]], cos, sin) * (dk ** -0.5) for h in hs]
    vb = [v_ref[:, vl[h]] for h in hs]
    state = [s_ref[0, h] for h in hs]
    scores = [_dot_nt(qb[h], kr[h].astype(BF16)) * dec_ref[h] for h in hs]
    cross = [_dot(qb[h], state[h].astype(BF16)) * qdec_ref[h] for h in hs]
    kv = [_dot((kr[h] * kdec_ref[h]).T.astype(BF16), vb[h]) for h in hs]
    for h in hs:
        s_ref[0, h] = gc_ref[h] * state[h] + kv[h]
    inner = [_dot(scores[h].astype(BF16), vb[h]) for h in hs]
    for h in hs:
        y_ref[:, vl[h]] = _group_norm_gate(inner[h] + cross[h], gate_ref[:, vl[h]], gng_ref[:, vl[h]],
                                           gnb_ref[:, vl[h]]).astype(y_ref.dtype)


def _ret_prompt(proj, proj_v, bsz, t, gc, cos, sin, dec, qdec, kdec, gn_g, gn_b, heads, dk, dv):
    m = proj.shape[0]
    c = dec.shape[1]
    nc = t // c
    qk, vw = heads * dk, heads * dv
    assert vw == 2 * qk
    tok = lambda col: (lambda b, ci: (b * nc + ci, col))
    fix2 = lambda b, ci: (0, 0)
    fix3 = lambda b, ci: (0, 0, 0)
    return pl.pallas_call(
        functools.partial(_ret_prompt_kernel, heads=heads, dk=dk, dv=dv),
        grid=(bsz, nc),
        in_specs=[pl.BlockSpec(memory_space=pltpu.SMEM),
                  pl.BlockSpec((c, qk), tok(0)), pl.BlockSpec((c, qk), tok(1)),
                  pl.BlockSpec((c, vw), tok(0)), pl.BlockSpec((c, vw), tok(1)),
                  pl.BlockSpec((c, dk), lambda b, ci: (ci, 0)), pl.BlockSpec((c, dk), lambda b, ci: (ci, 0)),
                  pl.BlockSpec(dec.shape, fix3), pl.BlockSpec(qdec.shape, fix3), pl.BlockSpec(kdec.shape, fix3),
                  pl.BlockSpec((1, vw), fix2), pl.BlockSpec((1, vw), fix2)],
        out_specs=[pl.BlockSpec((c, vw), lambda b, ci: (b * nc + ci, 0)),
                   pl.BlockSpec((1, heads, dk, dv), lambda b, ci: (b, 0, 0, 0))],
        out_shape=[jax.ShapeDtypeStruct((m, vw), BF16), jax.ShapeDtypeStruct((bsz, heads, dk, dv), F32)],
        compiler_params=_params("parallel", "arbitrary"),
        name="ret_prompt",
    )(gc, proj, proj, proj_v, proj, cos, sin, dec, qdec, kdec, gn_g, gn_b)


def _ret_sample_kernel(gamma_ref, p_ref, pv_ref, s_ref, cos_ref, sin_ref, gng_ref, gnb_ref, y_ref, so_ref,
                       *, heads, dk, dv):
    qk = heads * dk
    cos = cos_ref[...]
    sin = sin_ref[...]

    def column(x):
        col = jnp.broadcast_to(x, (dk, dk)).T
        return jnp.concatenate([col] * (dv // dk), axis=1)

    hs = range(heads)
    vl = [slice(h * dv, (h + 1) * dv) for h in hs]
    q = [_rotate(p_ref[0, :, h * dk:(h + 1) * dk], cos, sin) for h in hs]
    k = [_rotate(p_ref[0, :, qk + h * dk:qk + (h + 1) * dk], cos, sin) * (dk ** -0.5) for h in hs]
    v = [pv_ref[0, :, vl[h]].astype(F32) for h in hs]
    q_col = [column(x) for x in q]
    k_col = [column(x) for x in k]
    qk_dot = [jnp.sum(q[h] * k[h], axis=-1, keepdims=True) for h in hs]
    for h in hs:
        state = s_ref[0, h]
        o = qk_dot[h] * v[h] + jnp.sum(q_col[h] * state, axis=0, keepdims=True) * gamma_ref[h]
        so_ref[0, h] = gamma_ref[h] * state + k_col[h] * v[h]
        gate = p_ref[0, :, 2 * qk + h * dv:2 * qk + (h + 1) * dv]
        y_ref[0, :, vl[h]] = _group_norm_gate(o, gate, gng_ref[:, vl[h]], gnb_ref[:, vl[h]])


def _ret_sample(proj, proj_v, state, gamma, cos, sin, gn_g, gn_b, heads, dk, dv):
    s, n = proj.shape
    vw = heads * dv
    p3 = proj.reshape(s, 1, n)
    pv3 = proj_v.reshape(s, 1, vw)
    fix2 = lambda i: (0, 0)
    y, new_state = pl.pallas_call(
        functools.partial(_ret_sample_kernel, heads=heads, dk=dk, dv=dv),
        grid=(s,),
        in_specs=[pl.BlockSpec(memory_space=pltpu.SMEM),
                  pl.BlockSpec((1, 1, n), lambda i: (i, 0, 0)),
                  pl.BlockSpec((1, 1, vw), lambda i: (i, 0, 0)),
                  pl.BlockSpec((1, heads, dk, dv), lambda i: (i, 0, 0, 0)),
                  pl.BlockSpec((1, dk), fix2), pl.BlockSpec((1, dk), fix2),
                  pl.BlockSpec((1, vw), fix2), pl.BlockSpec((1, vw), fix2)],
        out_specs=[pl.BlockSpec((1, 1, vw), lambda i: (i, 0, 0)),
                   pl.BlockSpec((1, heads, dk, dv), lambda i: (i, 0, 0, 0))],
        out_shape=[jax.ShapeDtypeStruct((s, 1, vw), F32), jax.ShapeDtypeStruct(state.shape, F32)],
        compiler_params=_params("parallel"),
        name="ret_sample",
    )(gamma, p3, pv3, state, cos, sin, gn_g, gn_b)
    return y.reshape(s, vw), new_state


def _rope_tables(pos, dk):
    half = dk // 2
    inv_freq = ROPE_BASE ** (-jnp.linspace(0.0, 1.0, half, dtype=F32))
    ang = pos.astype(F32)[:, None] * inv_freq[None, :]
    cos, sin = jnp.cos(ang), jnp.sin(ang)
    return jnp.concatenate([cos, cos], axis=1), jnp.concatenate([-sin, sin], axis=1)


def _decay_tables(c, dk, dv):
    log_g = jnp.log1p(-jnp.exp2(-5.0 - jnp.arange(RET_HEADS, dtype=F32)))
    idx = jnp.arange(c, dtype=F32)
    diff = idx[:, None] - idx[None, :]
    dec = jnp.where(diff >= 0, jnp.exp(log_g[:, None, None] * jnp.maximum(diff, 0.0)), 0.0)
    qdec = jnp.exp(log_g[:, None] * (idx[None, :] + 1.0))
    kdec = jnp.exp(log_g[:, None] * (c - 1.0 - idx[None, :]))
    qdec = jnp.broadcast_to(qdec[:, :, None], (RET_HEADS, c, dv))
    kdec = jnp.broadcast_to(kdec[:, :, None], (RET_HEADS, c, dk))
    return dec, qdec, kdec, jnp.exp(log_g * c), jnp.exp(log_g)


def kernel(x_prompt, x_sample, cache_sb_k, cache_sb_v, cache_conv, state_ret, page_table, norm_mix_even, w_in_even, conv_w, conv_b, conv_ln_g, conv_ln_b, sb_q_norm, sb_k_norm, sb_bias, w_out_even, norm_ffn_even, ffn_w_gate, ffn_w_up, ffn_w_down, norm_mix_odd, w_in_odd, ret_gn_g, ret_gn_b, w_out_odd, norm_ffn_odd, router_w, moe_w_gate, moe_w_up, moe_w_down):
    bsz, t, d = x_prompt.shape
    s = x_sample.shape[0]
    assert x_sample.shape[1] == 1
    n_layers = w_in_even.shape[0] + w_in_odd.shape[0]
    heads, dh = sb_bias.shape[1], sb_q_norm.shape[1]
    w = heads * dh
    c = conv_w.shape[2]
    cw = conv_w.shape[1]
    page = cache_sb_k.shape[2]
    n_ctx = page_table.shape[1] * page
    dk = d // RET_HEADS
    dv = 2 * dk
    n_exp = router_w.shape[2]

    yp = x_prompt.reshape(bsz * t, d)
    ys = x_sample.reshape(s, d)
    row = lambda a: a.reshape(1, -1)

    head_mean = jnp.kron(jnp.eye(heads, dtype=F32), jnp.full((dh, dh), 1.0 / dh, F32)).astype(BF16)
    cos_p, sin_p = _rope_tables(jnp.arange(t, dtype=jnp.int32), dk)
    cos_s, sin_s = _rope_tables(jnp.full((1,), n_ctx, jnp.int32), dk)
    chunk = RET_CHUNK if t % RET_CHUNK == 0 else t
    dec, qdec, kdec, gamma_chunk, gamma = _decay_tables(chunk, dk, dv)

    kp_l, vp_l, ks_l, vs_l, cp_l, cs_l, rp_l, rs_l = [], [], [], [], [], [], [], []
    for layer in range(n_layers):
        li = layer // 2
        if layer % 2 == 0:
            w_in = w_in_even[li][:, :2 * c + w].astype(BF16)
            wkv_t = w_in_even[li][:, 2 * c + w:].T.astype(BF16)
            qg = row(jnp.tile(sb_q_norm[li], heads))
            kg = sb_k_norm[li].reshape(dh, 1)
            w_out = w_out_even[li].astype(BF16)
            conv_args = (conv_w[li], row(conv_b[li]), row(conv_ln_g[li]), row(conv_ln_b[li]))
            g_mix = row(norm_mix_even[li])

            even_w = (g_mix, w_in, wkv_t, qg, kg, head_mean, c, w, dh)
            up, qpb, kp, vp, kpb, vpb = _even_in(yp, bsz, *even_w, 512)
            us, qsb, k_s, v_s, _, _ = _even_in(ys, 1, *even_w, 128)

            up3 = up.reshape(bsz, t, c)
            full = jnp.pad(up3, ((0, 0), (CONV_HEAD, 0), (0, 0)))
            cp = _conv_prompt(full, *conv_args, t, 256).reshape(bsz * t, c)
            cs = _conv_sample(cache_conv[li], us, *conv_args)

            op = _sb_prompt(sb_bias[li], qpb.reshape(bsz, t, w), kpb, vpb, heads, dh, 256).reshape(bsz * t, w)
            n_pool = cache_sb_k.shape[1]
            n_pages = page_table.shape[1]
            pool_kt = cache_sb_k[li].transpose(0, 2, 3, 1).reshape(n_pool, w, page)
            pool_vt = cache_sb_v[li].transpose(0, 2, 3, 1).reshape(n_pool, w, page)
            os_ = _sb_decode(page_table, qsb, jnp.tile(sb_bias[li], n_pages).reshape(n_pages * heads, 1),
                             pool_kt, pool_vt, heads, dh)

            yp = _out_proj(yp, [cp, op], [w_out[:c], w_out[c:]], 1024)
            ys = _out_proj(ys, [cs, os_], [w_out[:c], w_out[c:]], 128)

            ffn_w = (ffn_w_gate[li].astype(BF16), ffn_w_up[li].astype(BF16), ffn_w_down[li].astype(BF16))
            g_ffn = row(norm_ffn_even[li])
            yp = _ffn(yp, g_ffn, *ffn_w, 1024, 256)
            ys = _ffn(ys, g_ffn, *ffn_w, 128, 256)

            kp_l.append(kp.reshape(bsz, heads, dh, t).transpose(0, 3, 1, 2))
            vp_l.append(vp.reshape(bsz, heads, dh, t).transpose(0, 3, 1, 2))
            ks_l.append(k_s.reshape(heads, dh, s).transpose(2, 0, 1).reshape(s, 1, heads, dh))
            vs_l.append(v_s.reshape(heads, dh, s).transpose(2, 0, 1).reshape(s, 1, heads, dh))
            cp_l.append(up3[:, t - (cw - 1):])
            cs_l.append(jnp.concatenate([cache_conv[li][:, 1:], us[:, None, :]], axis=1))
        else:
            qk, vw = RET_HEADS * dk, RET_HEADS * dv
            n_qkg = 2 * qk + vw
            w_odd = w_in_odd[li]
            w_in = jnp.concatenate([w_odd[:, :2 * qk], w_odd[:, 2 * qk + vw:], w_odd[:, 2 * qk:2 * qk + vw]],
                                   axis=1).astype(BF16)
            w_out = w_out_odd[li].astype(BF16)
            g_mix = row(norm_mix_odd[li])
            gn_g, gn_b = row(ret_gn_g[li]), row(ret_gn_b[li])

            pp, pvp = _norm_proj(yp, g_mix, w_in, n_qkg, 1024, vw)
            ps, pvs = _norm_proj(ys, g_mix, w_in, n_qkg, 128, vw)
            mp, sp = _ret_prompt(pp, pvp, bsz, t, gamma_chunk, cos_p, sin_p, dec, qdec, kdec, gn_g, gn_b,
                                 RET_HEADS, dk, dv)
            ms, ss = _ret_sample(ps, pvs, state_ret[li], gamma, cos_s, sin_s, gn_g, gn_b, RET_HEADS, dk, dv)
            yp = _out_proj(yp, [mp], [w_out], 1024)
            ys = _out_proj(ys, [ms], [w_out], 128)

            g_ffn = row(norm_ffn_odd[li])
            rw = jnp.pad(router_w[li], ((0, 0), (0, INFO_LANES - n_exp)))
            rw_hi = rw.astype(BF16)
            rw_lo = (rw - rw_hi.astype(F32)).astype(BF16)
            moe_w = (moe_w_gate[li].astype(BF16), moe_w_up[li].astype(BF16), moe_w_down[li].astype(BF16))
            yp = _moe(yp, g_ffn, rw_hi, rw_lo, *moe_w, n_exp, 512, 512, 512)
            ys = _moe(ys, g_ffn, rw_hi, rw_lo, *moe_w, n_exp, 128, 512, 1792)
            rp_l.append(sp)
            rs_l.append(ss)

    return (yp.reshape(bsz, t, d), ys.reshape(s, 1, d), jnp.stack(kp_l), jnp.stack(vp_l), jnp.stack(ks_l),
            jnp.stack(vs_l), jnp.stack(cp_l), jnp.stack(cs_l), jnp.stack(rp_l), jnp.stack(rs_l))
```

```python
import functools

import jax
import jax.numpy as jnp
from jax import lax
from jax.experimental import pallas as pl
from jax.experimental.pallas import tpu as pltpu

F32 = jnp.float32
BF16 = jnp.bfloat16

EPS = 1e-6
RET_HEADS = 8
RET_CHUNK = 128
ROPE_BASE = 10000.0
TOP_K = 2

V7X_VMEM_LIMIT_BYTES = 56 * 1024 * 1024


def _params(*sem):
    return pltpu.CompilerParams(dimension_semantics=sem, vmem_limit_bytes=V7X_VMEM_LIMIT_BYTES)


def _row_tile(m, want):
    t = min(m, want)
    while m % t:
        t //= 2
    return t


def _rms(x, g):
    return x * lax.rsqrt(jnp.mean(x * x, axis=-1, keepdims=True) + EPS) * g


def _split_bf16(x):
    hi = x.astype(BF16)
    lo = (x - hi.astype(F32)).astype(BF16)
    return hi, lo


LOG2_E = 1.4426950408889634


def _log2_sigmoid(z2):
    return jnp.minimum(z2, 0.0) - jnp.log2(1.0 + jnp.exp2(-jnp.abs(z2)))


def _dot(a, b):
    return jnp.dot(a, b, preferred_element_type=F32)


def _dot_nt(a, b):
    return lax.dot_general(a, b, (((1,), (1,)), ((), ())), preferred_element_type=F32)


def _even_in_kernel(x_ref, g_ref, w_ref, wkv_t_ref, qg_ref, kg_ref, hm_ref,
                    u_ref, qb_ref, kt_ref, vt_ref, ktb_ref, vtb_ref, *, c, w, dh, qscale):
    hb = _rms(x_ref[...], g_ref[...]).astype(BF16)
    a = _dot(hb, w_ref[:, 0:c])
    gate = _dot(hb, w_ref[:, c:2 * c])
    u_ref[...] = a * jax.nn.sigmoid(gate)
    q = _dot(hb, w_ref[:, 2 * c:2 * c + w])
    hi, lo = _split_bf16(q * q)
    ms = _dot(hi, hm_ref[...]) + _dot(lo, hm_ref[...])
    qb_ref[...] = (q * lax.rsqrt(ms + EPS) * (qg_ref[...] * qscale)).astype(BF16)
    kt = _dot_nt(wkv_t_ref[0:w, :], hb)
    for h in range(w // dh):
        rows = slice(h * dh, (h + 1) * dh)
        kh = kt[rows]
        kh = kh * lax.rsqrt(jnp.mean(kh * kh, axis=0, keepdims=True) + EPS) * kg_ref[...]
        kt_ref[0, rows, :] = kh
        ktb_ref[0, rows, :] = kh.astype(BF16)
    vt = _dot_nt(wkv_t_ref[w:2 * w, :], hb)
    vt_ref[0] = vt
    vtb_ref[0] = vt.astype(BF16)


def _even_in(x, n_seq, g, w_bf, wkv_t, qg_t, kg_col, hm, c, w, dh, tm):
    m, d = x.shape
    t = m // n_seq
    tm = _row_tile(t, tm)
    nt = t // tm
    row = lambda i: (i, 0)
    fix = lambda i: (0, 0)
    col = lambda i: (i // nt, 0, i % nt)
    out_t = lambda dt: jax.ShapeDtypeStruct((n_seq, w, t), dt)
    return pl.pallas_call(
        functools.partial(_even_in_kernel, c=c, w=w, dh=dh, qscale=dh ** -0.5 * LOG2_E),
        grid=(m // tm,),
        in_specs=[pl.BlockSpec((tm, d), row), pl.BlockSpec((1, d), fix), pl.BlockSpec(w_bf.shape, fix),
                  pl.BlockSpec(wkv_t.shape, fix), pl.BlockSpec((1, w), fix), pl.BlockSpec((dh, 1), fix),
                  pl.BlockSpec((w, w), fix)],
        out_specs=[pl.BlockSpec((tm, c), row), pl.BlockSpec((tm, w), row)] + [pl.BlockSpec((1, w, tm), col)] * 4,
        out_shape=[jax.ShapeDtypeStruct((m, c), F32), jax.ShapeDtypeStruct((m, w), BF16),
                   out_t(F32), out_t(F32), out_t(BF16), out_t(BF16)],
        compiler_params=_params("parallel"),
        name="even_in",
    )(x, g, w_bf, wkv_t, qg_t, kg_col, hm)


CONV_HEAD = 32


def _ln_silu(cv, g, b):
    mu = jnp.mean(cv, axis=-1, keepdims=True)
    var = jnp.mean(jnp.square(cv - mu), axis=-1, keepdims=True)
    y = (cv - mu) * lax.rsqrt(var + EPS) * g + b
    return y * jax.nn.sigmoid(y)


def _conv_prompt_kernel(full_ref, w_ref, b_ref, g_ref, beta_ref, o_ref, *, tt, cw):
    t = pl.program_id(1)
    n = tt + CONV_HEAD
    win = full_ref[0, pl.ds(pl.multiple_of(t * tt, tt), n), :]
    acc = jnp.zeros((tt, win.shape[1]), F32)
    for j in range(cw):
        off = j + CONV_HEAD - (cw - 1)
        shifted = win if off == 0 else pltpu.roll(win, n - off, axis=0)
        acc = acc + shifted[:tt] * w_ref[j:j + 1, :]
    o_ref[0] = _ln_silu(acc + b_ref[...], g_ref[...], beta_ref[...]).astype(o_ref.dtype)


def _conv_prompt(full, w, b, g, beta, t_len, tt):
    bsz, tf, c = full.shape
    cw = w.shape[0]
    tt = _row_tile(t_len, tt)
    fix = lambda i, t: (0, 0)
    return pl.pallas_call(
        functools.partial(_conv_prompt_kernel, tt=tt, cw=cw),
        grid=(bsz, t_len // tt),
        in_specs=[pl.BlockSpec((1, tf, c), lambda i, t: (i, 0, 0)), pl.BlockSpec((cw, c), fix),
                  pl.BlockSpec((1, c), fix), pl.BlockSpec((1, c), fix), pl.BlockSpec((1, c), fix)],
        out_specs=pl.BlockSpec((1, tt, c), lambda i, t: (i, t, 0)),
        out_shape=jax.ShapeDtypeStruct((bsz, t_len, c), BF16),
        compiler_params=_params("parallel", "arbitrary"),
        name="conv_prompt",
    )(full, w, b, g, beta)


def _conv_sample_kernel(buf_ref, u_ref, w_ref, b_ref, g_ref, beta_ref, o_ref, *, cw):
    past = jnp.sum(buf_ref[...] * w_ref[0:cw - 1, :][None], axis=1)
    cv = past + u_ref[...] * w_ref[cw - 1:cw, :] + b_ref[...]
    o_ref[...] = _ln_silu(cv, g_ref[...], beta_ref[...])


def _conv_sample(buf, u, w, b, g, beta):
    s, nb, c = buf.shape
    cw = w.shape[0]
    ts = _row_tile(s, 8)
    fix = lambda i: (0, 0)
    return pl.pallas_call(
        functools.partial(_conv_sample_kernel, cw=cw),
        grid=(s // ts,),
        in_specs=[pl.BlockSpec((ts, nb, c), lambda i: (i, 0, 0)), pl.BlockSpec((ts, c), lambda i: (i, 0)),
                  pl.BlockSpec((cw, c), fix), pl.BlockSpec((1, c), fix), pl.BlockSpec((1, c), fix),
                  pl.BlockSpec((1, c), fix)],
        out_specs=pl.BlockSpec((ts, c), lambda i: (i, 0)),
        out_shape=jax.ShapeDtypeStruct((s, c), F32),
        compiler_params=_params("parallel"),
        name="conv_sample",
    )(buf, u, w, b, g, beta)


def _strict_lower_ones(n, repeat=1):
    r = lax.broadcasted_iota(jnp.int32, (repeat * n, n), 0) % n
    c = lax.broadcasted_iota(jnp.int32, (repeat * n, n), 1)
    return jnp.where(r > c, 1.0, 0.0).astype(BF16)


def _suffix_after(lk, ones_after2):
    return _dot(jnp.concatenate(_split_bf16(lk), axis=1), ones_after2)


def _sb_blocks(qs, k_blks, v_blks, biases2, runs, ones_after2, mask):
    n = len(qs)
    z2 = [_dot(qs[g], k_blks[g]) + biases2[g] for g in range(n)]
    ls = [_log2_sigmoid(z) for z in z2]
    lk = [l - z for l, z in zip(ls, z2)]
    if mask is not None:
        lk = [jnp.where(mask, x, 0.0) for x in lk]
    suf = [_suffix_after(x, ones_after2) for x in lk]
    a = [jnp.exp2(ls[g] + suf[g] + runs[g]) for g in range(n)]
    if mask is not None:
        a = [jnp.where(mask, x, 0.0) for x in a]
    pv = [_dot_nt(a[g].astype(BF16), v_blks[g]) for g in range(n)]
    return pv, [runs[g] + (suf[g][:, 0:1] + lk[g][:, 0:1]) for g in range(n)]


def _sb_prompt_kernel(bias_ref, q_ref, k_ref, v_ref, o_ref, *, tq, heads, dh, group):
    i = pl.program_id(1)
    row = lax.broadcasted_iota(jnp.int32, (tq, tq), 0)
    col = lax.broadcasted_iota(jnp.int32, (tq, tq), 1)
    causal = row > col
    ones_after = _strict_lower_ones(tq, repeat=2)
    for h0 in range(0, heads, group):
        lanes = [slice(h * dh, (h + 1) * dh) for h in range(h0, h0 + group)]
        qs = [q_ref[0, :, l] for l in lanes]
        biases = [bias_ref[h] * LOG2_E for h in range(h0, h0 + group)]

        def tile(j, carry, mask, lanes=lanes, qs=qs, biases=biases):
            keys = pl.ds(pl.multiple_of(j * tq, tq), tq)
            pv, runs = _sb_blocks(qs, [k_ref[0, l, keys] for l in lanes], [v_ref[0, l, keys] for l in lanes],
                                  biases, [c[1] for c in carry], ones_after, mask)
            return tuple((carry[g][0] + pv[g], runs[g]) for g in range(group))

        init = tuple((jnp.zeros((tq, dh), F32), jnp.zeros((tq, 1), F32)) for _ in range(group))
        carry = tile(i, init, causal)
        carry = lax.fori_loop(1, i + 1, lambda jj, c, tile=tile: tile(i - jj, c, None), carry)
        for g in range(group):
            o_ref[0, :, lanes[g]] = carry[g][0]


def _sb_prompt(bias, qb, ktb, vtb, heads, dh, tq):
    bsz, t, w = qb.shape
    tq = _row_tile(t, tq)
    seq = lambda b, i: (b, 0, 0)
    return pl.pallas_call(
        functools.partial(_sb_prompt_kernel, tq=tq, heads=heads, dh=dh, group=8),
        grid=(bsz, t // tq),
        in_specs=[pl.BlockSpec(memory_space=pltpu.SMEM),
                  pl.BlockSpec((1, tq, w), lambda b, i: (b, i, 0)),
                  pl.BlockSpec((1, w, t), seq), pl.BlockSpec((1, w, t), seq)],
        out_specs=pl.BlockSpec((1, tq, w), lambda b, i: (b, i, 0)),
        out_shape=jax.ShapeDtypeStruct((bsz, t, w), F32),
        compiler_params=_params("parallel", "arbitrary"),
        name="sb_prompt",
    )(bias, qb, ktb, vtb)


def _sb_decode_kernel(pt_ref, q_ref, bias_ref, *refs, n_pages, heads, dh):
    del pt_ref
    k_refs = refs[:n_pages]
    v_refs = refs[n_pages:2 * n_pages]
    o_ref = refs[2 * n_pages]
    w = heads * dh
    page = k_refs[0].shape[2]
    rows = n_pages * heads
    head_of_lane = lax.broadcasted_iota(jnp.int32, (heads, w), 1) // dh
    head_mask = head_of_lane == lax.broadcasted_iota(jnp.int32, (heads, w), 0)
    q_bd = jnp.where(head_mask, jnp.broadcast_to(q_ref[0].astype(F32), (heads, w)), 0.0).astype(BF16)
    z = jnp.concatenate([_dot(q_bd, k_refs[p][0].astype(BF16)) for p in range(n_pages)], axis=0)
    z2 = z + bias_ref[...] * LOG2_E
    ls = _log2_sigmoid(z2)
    lk = ls - z2
    hi, lo = _split_bf16(lk)
    ones_after = _strict_lower_ones(page)
    within = _dot(hi, ones_after) + _dot(lo, ones_after)
    all_ones = jnp.ones((page, page), BF16)
    t_hi, t_lo = _split_bf16(_dot(hi, all_ones) + _dot(lo, all_ones))
    r = lax.broadcasted_iota(jnp.int32, (rows, rows), 0)
    c = lax.broadcasted_iota(jnp.int32, (rows, rows), 1)
    later_page = jnp.where((r % heads == c % heads) & (c // heads > r // heads), 1.0, 0.0).astype(BF16)
    run = _dot(later_page, t_hi) + _dot(later_page, t_lo)
    a = jnp.exp2(ls + within + run).astype(BF16)
    acc = jnp.zeros((heads, w), F32)
    for p in range(n_pages):
        acc = acc + _dot_nt(a[p * heads:(p + 1) * heads], v_refs[p][0].astype(BF16))
    o_ref[0] = jnp.sum(jnp.where(head_mask, acc, 0.0), axis=0, keepdims=True)


def _sb_decode(page_table, qb, bias_rows, pool_kt, pool_vt, heads, dh):
    s, n_pages = page_table.shape
    _, w, page = pool_kt.shape
    q3 = qb.reshape(s, 1, w)

    def page_spec(p):
        return pl.BlockSpec((1, w, page), lambda i, pt, p=p: (pt[i, p], 0, 0))

    grid_spec = pltpu.PrefetchScalarGridSpec(
        num_scalar_prefetch=1,
        grid=(s,),
        in_specs=[pl.BlockSpec((1, 1, w), lambda i, pt: (i, 0, 0)),
                  pl.BlockSpec((n_pages * heads, 1), lambda i, pt: (0, 0))]
        + [page_spec(p) for p in range(n_pages)] * 2,
        out_specs=pl.BlockSpec((1, 1, w), lambda i, pt: (i, 0, 0)),
    )
    out = pl.pallas_call(
        functools.partial(_sb_decode_kernel, n_pages=n_pages, heads=heads, dh=dh),
        grid_spec=grid_spec,
        out_shape=jax.ShapeDtypeStruct((s, 1, w), F32),
        compiler_params=_params("parallel"),
        name="sb_decode",
    )(page_table, q3, bias_rows, *([pool_kt] * n_pages), *([pool_vt] * n_pages))
    return out.reshape(s, w)


def _out_proj_kernel(*refs, n_in):
    res_ref = refs[0]
    o_ref = refs[1 + 2 * n_in]
    acc = res_ref[...]
    for k in range(n_in):
        acc = acc + _dot(refs[1 + k][...].astype(BF16), refs[1 + n_in + k][...])
    o_ref[...] = acc


def _out_proj(res, xs, ws, tm):
    m, d = res.shape
    tm = _row_tile(m, tm)
    row = lambda i: (i, 0)
    fix = lambda i: (0, 0)
    return pl.pallas_call(
        functools.partial(_out_proj_kernel, n_in=len(xs)),
        grid=(m // tm,),
        in_specs=[pl.BlockSpec((tm, d), row)] + [pl.BlockSpec((tm, x.shape[1]), row) for x in xs]
        + [pl.BlockSpec(w.shape, fix) for w in ws],
        out_specs=pl.BlockSpec((tm, d), row),
        out_shape=jax.ShapeDtypeStruct((m, d), F32),
        compiler_params=_params("parallel"),
        name="out_proj",
    )(res, *xs, *ws)


def _swiglu_step(hb, wg, wu, wd):
    gate = _dot(hb, wg)
    act = gate * jax.nn.sigmoid(gate) * _dot(hb, wu)
    return _dot(act.astype(BF16), wd)


def _ffn_kernel(x_ref, g_ref, wg_ref, wu_ref, wd_ref, o_ref, h_ref, acc_ref):
    j = pl.program_id(1)

    @pl.when(j == 0)
    def _():
        h_ref[...] = _rms(x_ref[...], g_ref[...]).astype(BF16)
        acc_ref[...] = jnp.zeros_like(acc_ref)

    acc_ref[...] += _swiglu_step(h_ref[...], wg_ref[...], wu_ref[...], wd_ref[...])

    @pl.when(j == pl.num_programs(1) - 1)
    def _():
        o_ref[...] = x_ref[...] + acc_ref[...]


def _ffn(x, g, wg, wu, wd, tm, tf):
    m, d = x.shape
    f = wg.shape[1]
    tm = _row_tile(m, tm)
    tf = _row_tile(f, tf)
    row = lambda i, j: (i, 0)
    return pl.pallas_call(
        _ffn_kernel,
        grid=(m // tm, f // tf),
        in_specs=[pl.BlockSpec((tm, d), row), pl.BlockSpec((1, d), lambda i, j: (0, 0)),
                  pl.BlockSpec((d, tf), lambda i, j: (0, j)), pl.BlockSpec((d, tf), lambda i, j: (0, j)),
                  pl.BlockSpec((tf, d), lambda i, j: (j, 0))],
        out_specs=pl.BlockSpec((tm, d), row),
        out_shape=jax.ShapeDtypeStruct((m, d), F32),
        scratch_shapes=[pltpu.VMEM((tm, d), BF16), pltpu.VMEM((tm, d), F32)],
        compiler_params=_params("parallel", "arbitrary"),
        name="ffn",
    )(x, g, wg, wu, wd)


INFO_LANES = 128


def _router_kernel(x_ref, g_ref, whi_ref, wlo_ref, info_ref, cnt_ref, *, n_exp):
    @pl.when(pl.program_id(0) == 0)
    def _():
        cnt_ref[...] = jnp.zeros_like(cnt_ref)

    h = _rms(x_ref[...], g_ref[...])
    hi, lo = _split_bf16(h)
    logits = _dot(hi, whi_ref[...]) + (_dot(lo, whi_ref[...]) + _dot(hi, wlo_ref[...]))
    tm = logits.shape[0]
    lane = lax.broadcasted_iota(jnp.int32, logits.shape, 1).astype(F32)
    neg = jnp.float32(-jnp.inf)
    big = jnp.float32(logits.shape[1])
    l1 = jnp.where(lane < n_exp, logits, neg)
    m1 = jnp.max(l1, axis=1, keepdims=True)
    i1 = jnp.min(jnp.where(l1 == m1, lane, big), axis=1, keepdims=True)
    l2 = jnp.where(lane == i1, neg, l1)
    m2 = jnp.max(l2, axis=1, keepdims=True)
    i2 = jnp.min(jnp.where(l2 == m2, lane, big), axis=1, keepdims=True)
    e2 = jnp.exp(m2 - m1)
    den = 1.0 + e2
    sel = jnp.where(lane == i1, 1.0, 0.0) + jnp.where(lane == i2, 1.0, 0.0)
    r = lax.broadcasted_iota(jnp.int32, (tm, tm), 0)
    c = lax.broadcasted_iota(jnp.int32, (tm, tm), 1)
    earlier_rows = jnp.where(r > c, 1.0, 0.0).astype(BF16)
    before = _dot(earlier_rows, sel.astype(BF16)) + cnt_ref[...]
    r1 = jnp.sum(jnp.where(lane == i1, before, 0.0), axis=1, keepdims=True)
    r2 = jnp.sum(jnp.where(lane == i2, before, 0.0), axis=1, keepdims=True)
    fields = (i1, i2, 1.0 / den, e2 / den, r1, r2)
    info = jnp.zeros_like(logits)
    for k, f in enumerate(fields):
        info = jnp.where(lane == k, f, info)
    info_ref[...] = info
    cnt_ref[...] += jnp.sum(sel, axis=0, keepdims=True)


def _router(x, g, w_hi, w_lo, n_exp, tm):
    m, d = x.shape
    tm = _row_tile(m, tm)
    row = lambda i: (i, 0)
    fix = lambda i: (0, 0)
    return pl.pallas_call(
        functools.partial(_router_kernel, n_exp=n_exp),
        grid=(m // tm,),
        in_specs=[pl.BlockSpec((tm, d), row), pl.BlockSpec((1, d), fix),
                  pl.BlockSpec((d, INFO_LANES), fix), pl.BlockSpec((d, INFO_LANES), fix)],
        out_specs=[pl.BlockSpec((tm, INFO_LANES), row), pl.BlockSpec((1, INFO_LANES), fix)],
        out_shape=[jax.ShapeDtypeStruct((m, INFO_LANES), F32), jax.ShapeDtypeStruct((1, INFO_LANES), F32)],
        compiler_params=_params("arbitrary"),
        name="router",
    )(x, g, w_hi, w_lo)


def _routing_plan(info, cnt, n_exp, tg, n_tiles):
    i1, i2 = info[:, 0].astype(jnp.int32), info[:, 1].astype(jnp.int32)
    r1, r2 = info[:, 4].astype(jnp.int32), info[:, 5].astype(jnp.int32)
    counts = cnt[0, :n_exp].astype(jnp.int32)
    padded = (counts + tg - 1) // tg * tg
    ends = jnp.cumsum(padded)
    starts = ends - padded
    pos1 = jnp.take(starts, i1) + r1
    pos2 = jnp.take(starts, i2) + r2
    n_active = ends[n_exp - 1] // tg
    tiles = jnp.arange(n_tiles, dtype=jnp.int32)
    tile_expert = jnp.sum((tiles[:, None] * tg >= ends[None, :]).astype(jnp.int32), axis=1)
    tile_expert = jnp.minimum(tile_expert, jnp.take(tile_expert, jnp.maximum(n_active - 1, 0)))
    return pos1, pos2, tile_expert, n_active.reshape(1)


def _row_copies(copies_of, sem, n_rows):
    def start(r, carry):
        for src, dst in copies_of(r):
            pltpu.make_async_copy(src, dst, sem).start()
        return carry

    def wait(r, carry):
        for src, dst in copies_of(r):
            pltpu.make_async_copy(src, dst, sem).wait()
        return carry

    lax.fori_loop(0, n_rows, start, 0, unroll=8)
    lax.fori_loop(0, n_rows, wait, 0, unroll=8)


def _dispatch_kernel(pos1_ref, pos2_ref, x_ref, g_ref, init_ref, xs_ref, h_ref, sem):
    del init_ref
    tm = x_ref.shape[0]
    base = pl.program_id(0) * tm
    h_ref[...] = _rms(x_ref[...], g_ref[...])

    def copies_of(r):
        src = h_ref.at[pl.ds(r, 1), :]
        return [(src, xs_ref.at[pl.ds(pos_ref[base + r], 1), :]) for pos_ref in (pos1_ref, pos2_ref)]

    _row_copies(copies_of, sem, tm)


def _dispatch(x, g, pos1, pos2, n_slots, tm):
    m, d = x.shape
    tm = _row_tile(m, tm)
    grid_spec = pltpu.PrefetchScalarGridSpec(
        num_scalar_prefetch=2,
        grid=(m // tm,),
        in_specs=[pl.BlockSpec((tm, d), lambda i, p1, p2: (i, 0)), pl.BlockSpec((1, d), lambda i, p1, p2: (0, 0)),
                  pl.BlockSpec(memory_space=pl.ANY)],
        out_specs=pl.BlockSpec(memory_space=pl.ANY),
        scratch_shapes=[pltpu.VMEM((tm, d), F32), pltpu.SemaphoreType.DMA(())],
    )
    return pl.pallas_call(
        _dispatch_kernel,
        grid_spec=grid_spec,
        out_shape=jax.ShapeDtypeStruct((n_slots, d), F32),
        input_output_aliases={4: 0},
        compiler_params=_params("arbitrary"),
        name="moe_dispatch",
    )(pos1, pos2, x, g, jnp.zeros((n_slots, d), F32))


def _experts_kernel(te_ref, na_ref, x_ref, wg_ref, wu_ref, wd_ref, o_ref, h_ref, acc_ref):
    del te_ref
    i = pl.program_id(0)
    j = pl.program_id(1)
    last = pl.num_programs(1) - 1
    active = i < na_ref[0]

    @pl.when(active & (j == 0))
    def _():
        h_ref[...] = x_ref[...].astype(BF16)
        acc_ref[...] = jnp.zeros_like(acc_ref)

    @pl.when(active)
    def _():
        acc_ref[...] += _swiglu_step(h_ref[...], wg_ref[0], wu_ref[0], wd_ref[0])

    @pl.when(active & (j == last))
    def _():
        o_ref[...] = acc_ref[...]

    @pl.when(jnp.logical_not(active) & (j == last))
    def _():
        o_ref[...] = jnp.zeros_like(o_ref)


def _experts(xs, tile_expert, n_active, wg, wu, wd, tg, tf):
    n_slots, d = xs.shape
    f = wg.shape[2]
    tf = _row_tile(f, tf)
    nj = f // tf

    def col(i, j, te, na):
        return jnp.where(i < na[0], j, nj - 1)

    grid_spec = pltpu.PrefetchScalarGridSpec(
        num_scalar_prefetch=2,
        grid=(n_slots // tg, nj),
        in_specs=[pl.BlockSpec((tg, d), lambda i, j, te, na: (i, 0)),
                  pl.BlockSpec((1, d, tf), lambda i, j, te, na: (te[i], 0, col(i, j, te, na))),
                  pl.BlockSpec((1, d, tf), lambda i, j, te, na: (te[i], 0, col(i, j, te, na))),
                  pl.BlockSpec((1, tf, d), lambda i, j, te, na: (te[i], col(i, j, te, na), 0))],
        out_specs=pl.BlockSpec((tg, d), lambda i, j, te, na: (i, 0)),
        scratch_shapes=[pltpu.VMEM((tg, d), BF16), pltpu.VMEM((tg, d), F32)],
    )
    return pl.pallas_call(
        _experts_kernel,
        grid_spec=grid_spec,
        out_shape=jax.ShapeDtypeStruct((n_slots, d), F32),
        compiler_params=_params("parallel", "arbitrary"),
        name="moe_experts",
    )(tile_expert, n_active, xs, wg, wu, wd)


def _combine_kernel(pos1_ref, pos2_ref, x_ref, info_ref, ys_ref, o_ref, buf_ref, sem):
    tm = x_ref.shape[0]
    base = pl.program_id(0) * tm

    def copies_of(r):
        return [(ys_ref.at[pl.ds(pos_ref[base + r], 1), :], buf_ref.at[k, pl.ds(r, 1), :])
                for k, pos_ref in enumerate((pos1_ref, pos2_ref))]

    _row_copies(copies_of, sem, tm)
    o_ref[...] = x_ref[...] + info_ref[:, 2:3] * buf_ref[0] + info_ref[:, 3:4] * buf_ref[1]


def _combine(x, info, ys, pos1, pos2, tm):
    m, d = x.shape
    tm = _row_tile(m, tm)
    row = lambda i, p1, p2: (i, 0)
    grid_spec = pltpu.PrefetchScalarGridSpec(
        num_scalar_prefetch=2,
        grid=(m // tm,),
        in_specs=[pl.BlockSpec((tm, d), row), pl.BlockSpec((tm, INFO_LANES), row),
                  pl.BlockSpec(memory_space=pl.ANY)],
        out_specs=pl.BlockSpec((tm, d), row),
        scratch_shapes=[pltpu.VMEM((2, tm, d), F32), pltpu.SemaphoreType.DMA(())],
    )
    return pl.pallas_call(
        _combine_kernel,
        grid_spec=grid_spec,
        out_shape=jax.ShapeDtypeStruct((m, d), F32),
        compiler_params=_params("arbitrary"),
        name="moe_combine",
    )(pos1, pos2, x, info, ys)


def _moe(x, g, w_hi, w_lo, wg, wu, wd, n_exp, tm, tg, tf):
    m = x.shape[0]
    tg = min(tg, _row_tile(m, tg) * TOP_K)
    n_tiles = -(-(TOP_K * m) // tg) + n_exp
    info, cnt = _router(x, g, w_hi, w_lo, n_exp, tm)
    pos1, pos2, tile_expert, n_active = _routing_plan(info, cnt, n_exp, tg, n_tiles)
    xs = _dispatch(x, g, pos1, pos2, n_tiles * tg, tm)
    ys = _experts(xs, tile_expert, n_active, wg, wu, wd, tg, tf)
    return _combine(x, info, ys, pos1, pos2, tm)


def _norm_proj_kernel(x_ref, g_ref, w_ref, of_ref, ob_ref, h_ref, *, nf):
    j = pl.program_id(1)

    @pl.when(j == 0)
    def _():
        h_ref[...] = _rms(x_ref[...], g_ref[...]).astype(BF16)

    @pl.when(j < nf)
    def _():
        of_ref[...] = _dot(h_ref[...], w_ref[...])

    @pl.when(j >= nf)
    def _():
        ob_ref[...] = _dot(h_ref[...], w_ref[...]).astype(BF16)


def _norm_proj(x, g, w_bf, n_f32, tm, tn):
    m, d = x.shape
    n = w_bf.shape[1]
    tm = _row_tile(m, tm)
    assert n_f32 % tn == 0 and (n - n_f32) % tn == 0 and 0 < n_f32 < n
    nf = n_f32 // tn
    return pl.pallas_call(
        functools.partial(_norm_proj_kernel, nf=nf),
        grid=(m // tm, n // tn),
        in_specs=[pl.BlockSpec((tm, d), lambda i, j: (i, 0)), pl.BlockSpec((1, d), lambda i, j: (0, 0)),
                  pl.BlockSpec((d, tn), lambda i, j: (0, j))],
        out_specs=[pl.BlockSpec((tm, tn), lambda i, j: (i, jnp.minimum(j, nf - 1))),
                   pl.BlockSpec((tm, tn), lambda i, j: (i, jnp.maximum(j - nf, 0)))],
        out_shape=[jax.ShapeDtypeStruct((m, n_f32), F32), jax.ShapeDtypeStruct((m, n - n_f32), BF16)],
        scratch_shapes=[pltpu.VMEM((tm, d), BF16)],
        compiler_params=_params("parallel", "arbitrary"),
        name="norm_proj",
    )(x, g, w_bf)


def _rotate(x, cos, sin_signed):
    return x * cos + pltpu.roll(x, x.shape[1] // 2, axis=1) * sin_signed


def _group_norm_gate(o, gate, gn_g, gn_b):
    mu = jnp.mean(o, axis=-1, keepdims=True)
    var = jnp.mean(jnp.square(o - mu), axis=-1, keepdims=True)
    on = (o - mu) * lax.rsqrt(var + EPS) * gn_g + gn_b
    return gate * jax.nn.sigmoid(gate) * on


def _ret_prompt_kernel(gc_ref, q_ref, k_ref, v_ref, gate_ref, cos_ref, sin_ref, dec_ref, qdec_ref, kdec_ref,
                       gng_ref, gnb_ref, y_ref, s_ref, *, heads, dk, dv):
    @pl.when(pl.program_id(1) == 0)
    def _():
        s_ref[...] = jnp.zeros_like(s_ref)

    cos = cos_ref[...]
    sin = sin_ref[...]
    hs = range(heads)
    kl = [slice(h * dk, (h + 1) * dk) for h in hs]
    vl = [slice(h * dv, (h + 1) * dv) for h in hs]
    qb = [_rotate(q_ref[:, kl[h]], cos, sin).astype(BF16) for h in hs]
    kr = [_rotate(k_ref[:, kl[h]], cos, sin) * (dk ** -0.5) for h in hs]
    vb = [v_ref[:, vl[h]] for h in hs]
    state = [s_ref[0, h] for h in hs]
    scores = [_dot_nt(qb[h], kr[h].astype(BF16)) * dec_ref[h] for h in hs]
    cross = [_dot(qb[h], state[h].astype(BF16)) * qdec_ref[h] for h in hs]
    kv = [_dot((kr[h] * kdec_ref[h]).T.astype(BF16), vb[h]) for h in hs]
    for h in hs:
        s_ref[0, h] = gc_ref[h] * state[h] + kv[h]
    inner = [_dot(scores[h].astype(BF16), vb[h]) for h in hs]
    for h in hs:
        y_ref[:, vl[h]] = _group_norm_gate(inner[h] + cross[h], gate_ref[:, vl[h]], gng_ref[:, vl[h]],
                                           gnb_ref[:, vl[h]]).astype(y_ref.dtype)


def _ret_prompt(proj, proj_v, bsz, t, gc, cos, sin, dec, qdec, kdec, gn_g, gn_b, heads, dk, dv):
    m = proj.shape[0]
    c = dec.shape[1]
    nc = t // c
    qk, vw = heads * dk, heads * dv
    assert vw == 2 * qk
    tok = lambda col: (lambda b, ci: (b * nc + ci, col))
    fix2 = lambda b, ci: (0, 0)
    fix3 = lambda b, ci: (0, 0, 0)
    return pl.pallas_call(
        functools.partial(_ret_prompt_kernel, heads=heads, dk=dk, dv=dv),
        grid=(bsz, nc),
        in_specs=[pl.BlockSpec(memory_space=pltpu.SMEM),
                  pl.BlockSpec((c, qk), tok(0)), pl.BlockSpec((c, qk), tok(1)),
                  pl.BlockSpec((c, vw), tok(0)), pl.BlockSpec((c, vw), tok(1)),
                  pl.BlockSpec((c, dk), lambda b, ci: (ci, 0)), pl.BlockSpec((c, dk), lambda b, ci: (ci, 0)),
                  pl.BlockSpec(dec.shape, fix3), pl.BlockSpec(qdec.shape, fix3), pl.BlockSpec(kdec.shape, fix3),
                  pl.BlockSpec((1, vw), fix2), pl.BlockSpec((1, vw), fix2)],
        out_specs=[pl.BlockSpec((c, vw), lambda b, ci: (b * nc + ci, 0)),
                   pl.BlockSpec((1, heads, dk, dv), lambda b, ci: (b, 0, 0, 0))],
        out_shape=[jax.ShapeDtypeStruct((m, vw), BF16), jax.ShapeDtypeStruct((bsz, heads, dk, dv), F32)],
        compiler_params=_params("parallel", "arbitrary"),
        name="ret_prompt",
    )(gc, proj, proj, proj_v, proj, cos, sin, dec, qdec, kdec, gn_g, gn_b)


def _ret_sample_kernel(gamma_ref, p_ref, pv_ref, s_ref, cos_ref, sin_ref, gng_ref, gnb_ref, y_ref, so_ref,
                       *, heads, dk, dv):
    qk = heads * dk
    cos = cos_ref[...]
    sin = sin_ref[...]

    def column(x):
        col = jnp.broadcast_to(x, (dk, dk)).T
        return jnp.concatenate([col] * (dv // dk), axis=1)

    hs = range(heads)
    vl = [slice(h * dv, (h + 1) * dv) for h in hs]
    q = [_rotate(p_ref[0, :, h * dk:(h + 1) * dk], cos, sin) for h in hs]
    k = [_rotate(p_ref[0, :, qk + h * dk:qk + (h + 1) * dk], cos, sin) * (dk ** -0.5) for h in hs]
    v = [pv_ref[0, :, vl[h]].astype(F32) for h in hs]
    q_col = [column(x) for x in q]
    k_col = [column(x) for x in k]
    qk_dot = [jnp.sum(q[h] * k[h], axis=-1, keepdims=True) for h in hs]
    for h in hs:
        state = s_ref[0, h]
        o = qk_dot[h] * v[h] + jnp.sum(q_col[h] * state, axis=0, keepdims=True) * gamma_ref[h]
        so_ref[0, h] = gamma_ref[h] * state + k_col[h] * v[h]
        gate = p_ref[0, :, 2 * qk + h * dv:2 * qk + (h + 1) * dv]
        y_ref[0, :, vl[h]] = _group_norm_gate(o, gate, gng_ref[:, vl[h]], gnb_ref[:, vl[h]])


def _ret_sample(proj, proj_v, state, gamma, cos, sin, gn_g, gn_b, heads, dk, dv):
    s, n = proj.shape
    vw = heads * dv
    p3 = proj.reshape(s, 1, n)
    pv3 = proj_v.reshape(s, 1, vw)
    fix2 = lambda i: (0, 0)
    y, new_state = pl.pallas_call(
        functools.partial(_ret_sample_kernel, heads=heads, dk=dk, dv=dv),
        grid=(s,),
        in_specs=[pl.BlockSpec(memory_space=pltpu.SMEM),
                  pl.BlockSpec((1, 1, n), lambda i: (i, 0, 0)),
                  pl.BlockSpec((1, 1, vw), lambda i: (i, 0, 0)),
                  pl.BlockSpec((1, heads, dk, dv), lambda i: (i, 0, 0, 0)),
                  pl.BlockSpec((1, dk), fix2), pl.BlockSpec((1, dk), fix2),
                  pl.BlockSpec((1, vw), fix2), pl.BlockSpec((1, vw), fix2)],
        out_specs=[pl.BlockSpec((1, 1, vw), lambda i: (i, 0, 0)),
                   pl.BlockSpec((1, heads, dk, dv), lambda i: (i, 0, 0, 0))],
        out_shape=[jax.ShapeDtypeStruct((s, 1, vw), F32), jax.ShapeDtypeStruct(state.shape, F32)],
        compiler_params=_params("parallel"),
        name="ret_sample",
    )(gamma, p3, pv3, state, cos, sin, gn_g, gn_b)
    return y.reshape(s, vw), new_state


def _rope_tables(pos, dk):
    half = dk // 2
    inv_freq = ROPE_BASE ** (-jnp.linspace(0.0, 1.0, half, dtype=F32))
    ang = pos.astype(F32)[:, None] * inv_freq[None, :]
    cos, sin = jnp.cos(ang), jnp.sin(ang)
    return jnp.concatenate([cos, cos], axis=1), jnp.concatenate([-sin, sin], axis=1)


def _decay_tables(c, dk, dv):
    log_g = jnp.log1p(-jnp.exp2(-5.0 - jnp.arange(RET_HEADS, dtype=F32)))
    idx = jnp.arange(c, dtype=F32)
    diff = idx[:, None] - idx[None, :]
    dec = jnp.where(diff >= 0, jnp.exp(log_g[:, None, None] * jnp.maximum(diff, 0.0)), 0.0)
    qdec = jnp.exp(log_g[:, None] * (idx[None, :] + 1.0))
    kdec = jnp.exp(log_g[:, None] * (c - 1.0 - idx[None, :]))
    qdec = jnp.broadcast_to(qdec[:, :, None], (RET_HEADS, c, dv))
    kdec = jnp.broadcast_to(kdec[:, :, None], (RET_HEADS, c, dk))
    return dec, qdec, kdec, jnp.exp(log_g * c), jnp.exp(log_g)


def kernel(x_prompt, x_sample, cache_sb_k, cache_sb_v, cache_conv, state_ret, page_table, norm_mix_even, w_in_even, conv_w, conv_b, conv_ln_g, conv_ln_b, sb_q_norm, sb_k_norm, sb_bias, w_out_even, norm_ffn_even, ffn_w_gate, ffn_w_up, ffn_w_down, norm_mix_odd, w_in_odd, ret_gn_g, ret_gn_b, w_out_odd, norm_ffn_odd, router_w, moe_w_gate, moe_w_up, moe_w_down):
    bsz, t, d = x_prompt.shape
    s = x_sample.shape[0]
    assert x_sample.shape[1] == 1
    n_layers = w_in_even.shape[0] + w_in_odd.shape[0]
    heads, dh = sb_bias.shape[1], sb_q_norm.shape[1]
    w = heads * dh
    c = conv_w.shape[2]
    cw = conv_w.shape[1]
    page = cache_sb_k.shape[2]
    n_ctx = page_table.shape[1] * page
    dk = d // RET_HEADS
    dv = 2 * dk
    n_exp = router_w.shape[2]

    yp = x_prompt.reshape(bsz * t, d)
    ys = x_sample.reshape(s, d)
    row = lambda a: a.reshape(1, -1)

    head_mean = jnp.kron(jnp.eye(heads, dtype=F32), jnp.full((dh, dh), 1.0 / dh, F32)).astype(BF16)
    cos_p, sin_p = _rope_tables(jnp.arange(t, dtype=jnp.int32), dk)
    cos_s, sin_s = _rope_tables(jnp.full((1,), n_ctx, jnp.int32), dk)
    chunk = RET_CHUNK if t % RET_CHUNK == 0 else t
    dec, qdec, kdec, gamma_chunk, gamma = _decay_tables(chunk, dk, dv)

    kp_l, vp_l, ks_l, vs_l, cp_l, cs_l, rp_l, rs_l = [], [], [], [], [], [], [], []
    for layer in range(n_layers):
        li = layer // 2
        if layer % 2 == 0:
            w_in = w_in_even[li][:, :2 * c + w].astype(BF16)
            wkv_t = w_in_even[li][:, 2 * c + w:].T.astype(BF16)
            qg = row(jnp.tile(sb_q_norm[li], heads))
            kg = sb_k_norm[li].reshape(dh, 1)
            w_out = w_out_even[li].astype(BF16)
            conv_args = (conv_w[li], row(conv_b[li]), row(conv_ln_g[li]), row(conv_ln_b[li]))
            g_mix = row(norm_mix_even[li])

            even_w = (g_mix, w_in, wkv_t, qg, kg, head_mean, c, w, dh)
            up, qpb, kp, vp, kpb, vpb = _even_in(yp, bsz, *even_w, 512)
            us, qsb, k_s, v_s, _, _ = _even_in(ys, 1, *even_w, 128)

            up3 = up.reshape(bsz, t, c)
            full = jnp.pad(up3, ((0, 0), (CONV_HEAD, 0), (0, 0)))
            cp = _conv_prompt(full, *conv_args, t, 256).reshape(bsz * t, c)
            cs = _conv_sample(cache_conv[li], us, *conv_args)

            op = _sb_prompt(sb_bias[li], qpb.reshape(bsz, t, w), kpb, vpb, heads, dh, 256).reshape(bsz * t, w)
            n_pool = cache_sb_k.shape[1]
            n_pages = page_table.shape[1]
            pool_kt = cache_sb_k[li].transpose(0, 2, 3, 1).reshape(n_pool, w, page)
            pool_vt = cache_sb_v[li].transpose(0, 2, 3, 1).reshape(n_pool, w, page)
            os_ = _sb_decode(page_table, qsb, jnp.tile(sb_bias[li], n_pages).reshape(n_pages * heads, 1),
                             pool_kt, pool_vt, heads, dh)

            yp = _out_proj(yp, [cp, op], [w_out[:c], w_out[c:]], 1024)
            ys = _out_proj(ys, [cs, os_], [w_out[:c], w_out[c:]], 128)

            ffn_w = (ffn_w_gate[li].astype(BF16), ffn_w_up[li].astype(BF16), ffn_w_down[li].astype(BF16))
            g_ffn = row(norm_ffn_even[li])
            yp = _ffn(yp, g_ffn, *ffn_w, 1024, 256)
            ys = _ffn(ys, g_ffn, *ffn_w, 128, 256)

            kp_l.append(kp.reshape(bsz, heads, dh, t).transpose(0, 3, 1, 2))
            vp_l.append(vp.reshape(bsz, heads, dh, t).transpose(0, 3, 1, 2))
            ks_l.append(k_s.reshape(heads, dh, s).transpose(2, 0, 1).reshape(s, 1, heads, dh))
            vs_l.append(v_s.reshape(heads, dh, s).transpose(2, 0, 1).reshape(s, 1, heads, dh))
            cp_l.append(up3[:, t - (cw - 1):])
            cs_l.append(jnp.concatenate([cache_conv[li][:, 1:], us[:, None, :]], axis=1))
        else:
            qk, vw = RET_HEADS * dk, RET_HEADS * dv
            n_qkg = 2 * qk + vw
            w_odd = w_in_odd[li]
            w_in = jnp.concatenate([w_odd[:, :2 * qk], w_odd[:, 2 * qk + vw:], w_odd[:, 2 * qk:2 * qk + vw]],
                                   axis=1).astype(BF16)
            w_out = w_out_odd[li].astype(BF16)
            g_mix = row(norm_mix_odd[li])
            gn_g, gn_b = row(ret_gn_g[li]), row(ret_gn_b[li])

            pp, pvp = _norm_proj(yp, g_mix, w_in, n_qkg, 1024, vw)
            ps, pvs = _norm_proj(ys, g_mix, w_in, n_qkg, 128, vw)
            mp, sp = _ret_prompt(pp, pvp, bsz, t, gamma_chunk, cos_p, sin_p, dec, qdec, kdec, gn_g, gn_b,
                                 RET_HEADS, dk, dv)
            ms, ss = _ret_sample(ps, pvs, state_ret[li], gamma, cos_s, sin_s, gn_g, gn_b, RET_HEADS, dk, dv)
            yp = _out_proj(yp, [mp], [w_out], 1024)
            ys = _out_proj(ys, [ms], [w_out], 128)

            g_ffn = row(norm_ffn_odd[li])
            rw = jnp.pad(router_w[li], ((0, 0), (0, INFO_LANES - n_exp)))
            rw_hi = rw.astype(BF16)
            rw_lo = (rw - rw_hi.astype(F32)).astype(BF16)
            moe_w = (moe_w_gate[li].astype(BF16), moe_w_up[li].astype(BF16), moe_w_down[li].astype(BF16))
            yp = _moe(yp, g_ffn, rw_hi, rw_lo, *moe_w, n_exp, 512, 512, 1792)
            ys = _moe(ys, g_ffn, rw_hi, rw_lo, *moe_w, n_exp, 128, 512, 1792)
            rp_l.append(sp)
            rs_l.append(ss)

    return (yp.reshape(bsz, t, d), ys.reshape(s, 1, d), jnp.stack(kp_l), jnp.stack(vp_l), jnp.stack(ks_l),
            jnp.stack(vs_l), jnp.stack(cp_l), jnp.stack(cs_l), jnp.stack(rp_l), jnp.stack(rs_l))
```
